```python
import math
import jax, jax.numpy as jnp
from jax import lax
import numpy as np

D_MODEL = 1024
BATCH = 4
SEQ = 8192
DEPTH = 1

HEAD_DIM = 64
A_Q_HEADS = 8
A_KV_HEADS = 2
A_GROUP = A_Q_HEADS // A_KV_HEADS
B_HEADS = 4
B_V_DIM = 2 * HEAD_DIM
A_Q = A_Q_HEADS * HEAD_DIM
A_KV = A_KV_HEADS * HEAD_DIM
B_QK = B_HEADS * 2 * HEAD_DIM
B_V = B_HEADS * B_V_DIM
MIX_WIDTH = A_Q + B_V
IN_WIDTH = A_Q + 2 * A_KV + 2 * B_QK + B_V
D_FF = 4 * D_MODEL
GRID_W = 64
ROPE_THETA = 10000.0
REL_BUCKETS = 32
REL_MAX_DIST = 128
Q_BLOCK = 128
EPS = 1e-6

kernel_name = "hymba_style_gqa_diffattn_encoder_layer"


def rmsnorm(x, g):
    xf = x.astype(jnp.float32)
    y = xf * lax.rsqrt(jnp.mean(xf * xf, axis=-1, keepdims=True) + EPS)
    return (y * g.astype(jnp.float32)).astype(x.dtype)


def axial_rope_tables(seq_len, dtype):
    rows = seq_len // GRID_W
    row = jnp.broadcast_to(jnp.arange(rows)[:, None], (rows, GRID_W)).reshape(-1).astype(jnp.float32)
    col = jnp.broadcast_to(jnp.arange(GRID_W)[None, :], (rows, GRID_W)).reshape(-1).astype(jnp.float32)
    half = HEAD_DIM // 2
    inv_freq = 1.0 / (ROPE_THETA ** (jnp.arange(0, half, 2, dtype=jnp.float32) / half))
    ang_r = row[:, None] * inv_freq[None, :]
    ang_c = col[:, None] * inv_freq[None, :]
    return tuple(t.astype(dtype) for t in (jnp.cos(ang_r), jnp.sin(ang_r), jnp.cos(ang_c), jnp.sin(ang_c)))


def rope_rotate(x, cos, sin):
    x1, x2 = jnp.split(x, 2, axis=-1)
    c = cos[None, :, None, :]
    s = sin[None, :, None, :]
    return jnp.concatenate([x1 * c - x2 * s, x2 * c + x1 * s], axis=-1)


def apply_axial_rope(x, tables):
    cr, sr, cc, sc = tables
    xr, xc = jnp.split(x, 2, axis=-1)
    return jnp.concatenate([rope_rotate(xr, cr, sr), rope_rotate(xc, cc, sc)], axis=-1)


def t5_bucket(rel):
    nb = REL_BUCKETS // 2
    max_exact = nb // 2
    ret = (rel > 0).astype(jnp.int32) * nb
    n = jnp.abs(rel)
    nf = jnp.maximum(n, 1).astype(jnp.float32)
    large = max_exact + (jnp.log(nf / max_exact) / math.log(REL_MAX_DIST / max_exact) * (nb - max_exact)).astype(jnp.int32)
    large = jnp.minimum(large, nb - 1)
    return ret + jnp.where(n < max_exact, n, large)


def gqa_axial_attention(q, k, v):
    B, S = q.shape[0], q.shape[1]
    nblk = S // Q_BLOCK
    scale = HEAD_DIM ** -0.5
    qb = q.reshape(B, nblk, Q_BLOCK, A_KV_HEADS, A_GROUP, HEAD_DIM).transpose(1, 0, 3, 4, 2, 5)
    kt = k.transpose(0, 2, 1, 3)
    vt = v.transpose(0, 2, 1, 3)

    def block(qblk):
        s = jnp.einsum('bkgqd,bksd->bkgqs', qblk, kt).astype(jnp.float32) * scale
        p = jax.nn.softmax(s, axis=-1)
        return jnp.einsum('bkgqs,bksd->bkgqd', p.astype(vt.dtype), vt)

    o = lax.map(block, qb)
    return o.transpose(1, 0, 4, 2, 3, 5).reshape(B, S, A_Q)


def differential_attention(q, k, v, lam, rel_bias):
    B, S = q.shape[0], q.shape[1]
    nblk = S // Q_BLOCK
    scale = HEAD_DIM ** -0.5
    qb = q.reshape(B, nblk, Q_BLOCK, B_HEADS, 2, HEAD_DIM).transpose(1, 0, 3, 4, 2, 5)
    kt = k.transpose(0, 2, 3, 1, 4)
    vt = v.transpose(0, 2, 1, 3)
    starts = jnp.arange(nblk, dtype=jnp.int32) * Q_BLOCK
    k_pos = jnp.arange(S, dtype=jnp.int32)
    lam32 = lam.astype(jnp.float32)

    def block(args):
        qblk, start = args
        q_pos = start + jnp.arange(Q_BLOCK, dtype=jnp.int32)
        bucket = t5_bucket(k_pos[None, :] - q_pos[:, None])
        bias = rel_bias[bucket].astype(jnp.float32).transpose(2, 0, 1)
        s = jnp.einsum('bhcqd,bhcsd->bhcqs', qblk, kt).astype(jnp.float32) * scale + bias[None, :, None]
        p = jax.nn.softmax(s, axis=-1)
        attn = p[:, :, 0] - lam32 * p[:, :, 1]
        return jnp.einsum('bhqs,bhse->bhqe', attn.astype(vt.dtype), vt)

    o = lax.map(block, (qb, starts))
    return o.transpose(1, 0, 3, 2, 4).reshape(B, S, B_HEADS, B_V_DIM)


def setup_inputs(seed: int = 0) -> dict:
    key = jax.random.key(seed)
    ks = jax.random.split(key, 20)
    f32 = jnp.float32
    nrm = lambda k, shape, s: jax.random.normal(k, shape, f32) * s
    return {
        "x": nrm(ks[0], (BATCH, SEQ, D_MODEL), 1.0),
        "attn_norm_g": 1.0 + nrm(ks[1], (DEPTH, D_MODEL), 0.05),
        "w_in": nrm(ks[2], (DEPTH, D_MODEL, IN_WIDTH), D_MODEL ** -0.5),
        "a_q_norm_g": 1.0 + nrm(ks[3], (DEPTH, HEAD_DIM), 0.05),
        "a_k_norm_g": 1.0 + nrm(ks[4], (DEPTH, HEAD_DIM), 0.05),
        "b_q_norm_g": 1.0 + nrm(ks[5], (DEPTH, HEAD_DIM), 0.05),
        "b_k_norm_g": 1.0 + nrm(ks[6], (DEPTH, HEAD_DIM), 0.05),
        "lambda_q1": nrm(ks[7], (DEPTH, HEAD_DIM), 0.1),
        "lambda_k1": nrm(ks[8], (DEPTH, HEAD_DIM), 0.1),
        "lambda_q2": nrm(ks[9], (DEPTH, HEAD_DIM), 0.1),
        "lambda_k2": nrm(ks[10], (DEPTH, HEAD_DIM), 0.1),
        "b_subln_g": 1.0 + nrm(ks[11], (DEPTH, B_V_DIM), 0.05),
        "rel_bias": nrm(ks[12], (REL_BUCKETS, B_HEADS), 0.5),
        "w_out": nrm(ks[13], (DEPTH, MIX_WIDTH, D_MODEL), MIX_WIDTH ** -0.5),
        "mlp_norm_g": 1.0 + nrm(ks[14], (DEPTH, D_MODEL), 0.05),
        "w_up": nrm(ks[15], (DEPTH, D_MODEL, D_FF), D_MODEL ** -0.5),
        "w_down": nrm(ks[16], (DEPTH, D_FF, D_MODEL), D_FF ** -0.5),
    }


def reference(x, attn_norm_g, w_in, a_q_norm_g, a_k_norm_g, b_q_norm_g, b_k_norm_g,
              lambda_q1, lambda_k1, lambda_q2, lambda_k2, b_subln_g, rel_bias,
              w_out, mlp_norm_g, w_up, w_down):
    B, S, _ = x.shape
    rope_tables = axial_rope_tables(S, x.dtype)
    split_pts = np.cumsum([A_Q, A_KV, A_KV, B_QK, B_QK])[:].tolist()
    for l in range(DEPTH):
        h = rmsnorm(x, attn_norm_g[l])
        proj = jnp.einsum('bsd,de->bse', h, w_in[l])
        qa, ka, va, qb, kb, vb = jnp.split(proj, split_pts, axis=-1)

        qa = rmsnorm(qa.reshape(B, S, A_Q_HEADS, HEAD_DIM), a_q_norm_g[l])
        ka = rmsnorm(ka.reshape(B, S, A_KV_HEADS, HEAD_DIM), a_k_norm_g[l])
        qa = apply_axial_rope(qa, rope_tables).reshape(B, S, A_KV_HEADS, A_GROUP, HEAD_DIM)
        ka = apply_axial_rope(ka, rope_tables)
        va = va.reshape(B, S, A_KV_HEADS, HEAD_DIM)
        out_a = gqa_axial_attention(qa, ka, va)

        lambda_init = 0.8 - 0.6 * math.exp(-0.3 * l)
        lam = (jnp.exp(jnp.sum(lambda_q1[l].astype(jnp.float32) * lambda_k1[l].astype(jnp.float32)))
               - jnp.exp(jnp.sum(lambda_q2[l].astype(jnp.float32) * lambda_k2[l].astype(jnp.float32)))
               + lambda_init)
        qb = rmsnorm(qb.reshape(B, S, B_HEADS, 2, HEAD_DIM), b_q_norm_g[l])
        kb = rmsnorm(kb.reshape(B, S, B_HEADS, 2, HEAD_DIM), b_k_norm_g[l])
        vb = vb.reshape(B, S, B_HEADS, B_V_DIM)
        ob = differential_attention(qb, kb, vb, lam, rel_bias)
        out_b = (rmsnorm(ob, b_subln_g[l]) * (1.0 - lambda_init)).reshape(B, S, B_V)

        mixed = jnp.concatenate([out_a, out_b], axis=-1)
        x = x + jnp.einsum('bse,ed->bsd', mixed, w_out[l])

        h = rmsnorm(x, mlp_norm_g[l])
        u = jnp.einsum('bsd,df->bsf', h, w_up[l])
        u = jnp.square(jax.nn.relu(u))
        x = x + jnp.einsum('bsf,fd->bsd', u, w_down[l])
    return x
```

```python
import functools
import math

import jax
import jax.numpy as jnp
from jax import lax
from jax.experimental import pallas as pl
from jax.experimental.pallas import tpu as pltpu

HEAD_DIM = 64
A_Q_HEADS = 8
A_KV_HEADS = 2
A_GROUP = A_Q_HEADS // A_KV_HEADS
B_HEADS = 4
B_V_DIM = 2 * HEAD_DIM
A_Q = A_Q_HEADS * HEAD_DIM
A_KV = A_KV_HEADS * HEAD_DIM
B_QK = B_HEADS * 2 * HEAD_DIM
B_V = B_HEADS * B_V_DIM
GRID_W = 64
ROPE_THETA = 10000.0
REL_BUCKETS = 32
EPS = 1e-6
LOG2E = 1.4426950408889634
T5_THRESHOLDS = (12, 16, 23, 32, 46, 64, 91)

OFF_QA = 0
OFF_KA = OFF_QA + A_Q
OFF_VA = OFF_KA + A_KV
OFF_QB = OFF_VA + A_KV
OFF_KB = OFF_QB + B_QK
OFF_VB = OFF_KB + B_QK
IN_WIDTH = OFF_VB + B_V

V7X_LANES = 128
BF16_SUBLANES = 16
ONES_ROWS = BF16_SUBLANES
CHUNK = 512
TQ_A = 512
TQ_B = CHUNK
MLP_ROWS = 512
MLP_FCHUNK = 1024
VMEM_LIMIT = 48 * 1024 * 1024

BF16 = jnp.bfloat16
F32 = jnp.float32


def _bias_tile_kernel(tbl_ref, out_ref):
    h = pl.program_id(0)
    d = pl.program_id(1) - 1
    kk = lax.broadcasted_iota(jnp.int32, (CHUNK, CHUNK), 0)
    qq = lax.broadcasted_iota(jnp.int32, (CHUNK, CHUNK), 1)
    rel = d * CHUNK + kk - qq
    n = jnp.abs(rel)
    large = jnp.full_like(n, REL_BUCKETS // 4)
    for t in T5_THRESHOLDS:
        large = large + (n >= t).astype(jnp.int32)
    bucket = jnp.where(rel > 0, REL_BUCKETS // 2, 0) + jnp.where(n < REL_BUCKETS // 4, n, large)
    acc = jnp.zeros((CHUNK, CHUNK), F32)
    for b in range(REL_BUCKETS):
        acc = jnp.where(bucket == b, tbl_ref[h, b] * LOG2E, acc)
    out_ref[0, 0] = acc


def _bias_tiles(rel_bias_t):
    return pl.pallas_call(
        _bias_tile_kernel,
        grid=(B_HEADS, 3),
        in_specs=[pl.BlockSpec(memory_space=pltpu.SMEM)],
        out_specs=pl.BlockSpec((1, 1, CHUNK, CHUNK), lambda h, d: (h, d, 0, 0)),
        out_shape=jax.ShapeDtypeStruct((B_HEADS, 3, CHUNK, CHUNK), F32),
        name="bias_tiles",
    )(rel_bias_t)


def _in_proj_kernel(x_ref, g_ref, w_ref, gqa_ref, gka_ref, gqb_ref, gkb_ref,
                    cr_ref, sr_ref, cc_ref, sc_ref,
                    qa_ref, ka_ref, va_ref, qb_ref, kb_ref, vb_ref, pt_ref):
    x = x_ref[0]
    ms = jnp.mean(x * x, axis=-1, keepdims=True)
    h = (x * lax.rsqrt(ms + EPS) * g_ref[...]).astype(BF16)
    pt_ref[...] = lax.dot_general(w_ref[...], h, (((1,), (1,)), ((), ())),
                                  preferred_element_type=F32)
    cr, sr, cc, sc = cr_ref[...], sr_ref[...], cc_ref[...], sc_ref[...]
    quarter = HEAD_DIM // 4

    def norm(r0, gain_ref):
        t = pt_ref[r0:r0 + HEAD_DIM, :]
        m = jnp.mean(t * t, axis=0, keepdims=True)
        return t * lax.rsqrt(m + EPS) * gain_ref[...]

    def rope(y):
        x1r, x2r = y[0:quarter], y[quarter:2 * quarter]
        x1c, x2c = y[2 * quarter:3 * quarter], y[3 * quarter:]
        return jnp.concatenate(
            [x1r * cr - x2r * sr, x2r * cr + x1r * sr,
             x1c * cc - x2c * sc, x2c * cc + x1c * sc], axis=0)

    zeros = jnp.zeros((HEAD_DIM, CHUNK), BF16)
    ones = jnp.ones((ONES_ROWS, CHUNK), BF16)

    for hh in range(A_Q_HEADS):
        q = rope(norm(OFF_QA + hh * HEAD_DIM, gqa_ref)).astype(BF16)
        slot = hh // A_GROUP
        qa_ref[0, hh, slot * HEAD_DIM:(slot + 1) * HEAD_DIM, :] = q
        qa_ref[0, hh, (1 - slot) * HEAD_DIM:(2 - slot) * HEAD_DIM, :] = zeros
    kt = jnp.concatenate([rope(norm(OFF_KA + kv * HEAD_DIM, gka_ref))
                          for kv in range(A_KV_HEADS)], axis=0)
    ka_ref[0] = kt.T.astype(BF16)
    for kv in range(A_KV_HEADS):
        r0 = OFF_VA + kv * HEAD_DIM
        va_ref[0, kv, 0, 0:HEAD_DIM, :] = pt_ref[r0:r0 + HEAD_DIM, :].astype(BF16)
        va_ref[0, kv, 0, HEAD_DIM:HEAD_DIM + ONES_ROWS, :] = ones

    for hb in range(B_HEADS):
        for c in range(2):
            q = norm(OFF_QB + (hb * 2 + c) * HEAD_DIM, gqb_ref).astype(BF16)
            qb_ref[0, hb, c, c * HEAD_DIM:(c + 1) * HEAD_DIM, :] = q
            qb_ref[0, hb, c, (1 - c) * HEAD_DIM:(2 - c) * HEAD_DIM, :] = zeros
        kt = jnp.concatenate([norm(OFF_KB + (hb * 2 + c) * HEAD_DIM, gkb_ref)
                              for c in range(2)], axis=0)
        kb_ref[0, hb] = kt.T.astype(BF16)
        r0 = OFF_VB + hb * B_V_DIM
        vb_ref[0, hb, 0, 0:B_V_DIM, :] = pt_ref[r0:r0 + B_V_DIM, :].astype(BF16)
        vb_ref[0, hb, 0, B_V_DIM:B_V_DIM + ONES_ROWS, :] = ones


def _in_proj(x, g, w_t, gqa, gka, gqb, gkb, cr, sr, cc, sc):
    bsz, seq, d = x.shape
    ns = seq // CHUNK
    const = lambda shape: pl.BlockSpec(shape, lambda b, s: (0,) * len(shape))
    tab = pl.BlockSpec((HEAD_DIM // 4, CHUNK), lambda b, s: (0, s))
    out_shape = (
        jax.ShapeDtypeStruct((bsz, A_Q_HEADS, 2 * HEAD_DIM, seq), BF16),
        jax.ShapeDtypeStruct((bsz, seq, 2 * HEAD_DIM), BF16),
        jax.ShapeDtypeStruct((bsz, A_KV_HEADS, ns, HEAD_DIM + ONES_ROWS, CHUNK), BF16),
        jax.ShapeDtypeStruct((bsz, B_HEADS, 2, 2 * HEAD_DIM, seq), BF16),
        jax.ShapeDtypeStruct((bsz, B_HEADS, seq, 2 * HEAD_DIM), BF16),
        jax.ShapeDtypeStruct((bsz, B_HEADS, ns, B_V_DIM + ONES_ROWS, CHUNK), BF16),
    )
    out_specs = (
        pl.BlockSpec((1, A_Q_HEADS, 2 * HEAD_DIM, CHUNK), lambda b, s: (b, 0, 0, s)),
        pl.BlockSpec((1, CHUNK, 2 * HEAD_DIM), lambda b, s: (b, s, 0)),
        pl.BlockSpec((1, A_KV_HEADS, 1, HEAD_DIM + ONES_ROWS, CHUNK),
                     lambda b, s: (b, 0, s, 0, 0)),
        pl.BlockSpec((1, B_HEADS, 2, 2 * HEAD_DIM, CHUNK), lambda b, s: (b, 0, 0, 0, s)),
        pl.BlockSpec((1, B_HEADS, CHUNK, 2 * HEAD_DIM), lambda b, s: (b, 0, s, 0)),
        pl.BlockSpec((1, B_HEADS, 1, B_V_DIM + ONES_ROWS, CHUNK),
                     lambda b, s: (b, 0, s, 0, 0)),
    )
    return pl.pallas_call(
        _in_proj_kernel,
        grid=(bsz, ns),
        in_specs=[
            pl.BlockSpec((1, CHUNK, d), lambda b, s: (b, s, 0)),
            const((1, d)),
            const((IN_WIDTH, d)),
            const((HEAD_DIM, CHUNK)), const((HEAD_DIM, CHUNK)),
            const((HEAD_DIM, CHUNK)), const((HEAD_DIM, CHUNK)),
            tab, tab, tab, tab,
        ],
        out_specs=out_specs,
        out_shape=out_shape,
        scratch_shapes=[pltpu.VMEM((IN_WIDTH, CHUNK), F32)],
        compiler_params=pltpu.CompilerParams(
            dimension_semantics=("arbitrary", "arbitrary"),
            vmem_limit_bytes=VMEM_LIMIT),
        name="in_proj",
    )(x, g, w_t, gqa, gka, gqb, gkb, cr, sr, cc, sc)


def _softmax_step(s, shift, m, v_t, acc_ref, idx):
    m_new = jnp.maximum(m, jnp.max(s, axis=0, keepdims=True) + shift)
    alpha = jnp.exp2(m - m_new)
    p = jnp.exp2(s - (m_new - shift)).astype(BF16)
    pv = jnp.dot(v_t, p, preferred_element_type=F32)
    acc_ref[idx] = alpha * acc_ref[idx] + pv
    return m_new


def _attn_a_kernel(q_ref, k_ref, v_ref, o_ref, acc_ref):
    nk = v_ref.shape[2]
    q_t = q_ref[0, 0]
    acc_ref[...] = jnp.zeros_like(acc_ref)

    def body(j, m):
        off = pl.multiple_of(j * CHUNK, CHUNK)
        k = k_ref[0, pl.ds(off, CHUNK), :]
        s = jnp.dot(k, q_t, preferred_element_type=F32)
        return _softmax_step(s, 0.0, m, v_ref[0, 0, j], acc_ref, 0)

    lax.fori_loop(0, nk, body, jnp.full((1, TQ_A), -jnp.inf, F32))
    acc = acc_ref[0]
    o_ref[0, 0] = (acc[0:HEAD_DIM] / acc[HEAD_DIM:HEAD_DIM + 1]).astype(BF16)


def _attn_a(qa, ka, va):
    bsz, _, _, seq = qa.shape
    ns = seq // CHUNK
    return pl.pallas_call(
        _attn_a_kernel,
        grid=(bsz, A_Q_HEADS, seq // TQ_A),
        in_specs=[
            pl.BlockSpec((1, 1, 2 * HEAD_DIM, TQ_A), lambda b, h, q: (b, h, 0, q)),
            pl.BlockSpec((1, seq, 2 * HEAD_DIM), lambda b, h, q: (b, 0, 0)),
            pl.BlockSpec((1, 1, ns, HEAD_DIM + ONES_ROWS, CHUNK),
                         lambda b, h, q: (b, h // A_GROUP, 0, 0, 0)),
        ],
        out_specs=pl.BlockSpec((1, 1, HEAD_DIM, TQ_A), lambda b, h, q: (b, h, 0, q)),
        out_shape=jax.ShapeDtypeStruct((bsz, A_Q_HEADS, HEAD_DIM, seq), BF16),
        scratch_shapes=[pltpu.VMEM((1, HEAD_DIM + ONES_ROWS, TQ_A), F32)],
        compiler_params=pltpu.CompilerParams(
            dimension_semantics=("arbitrary", "arbitrary", "arbitrary"),
            vmem_limit_bytes=VMEM_LIMIT),
        name="attn_a",
    )(qa, ka, va)


def _attn_b_kernel(tbl_ref, q_ref, k_ref, v_ref, bias_ref, lq1_ref, lk1_ref, lq2_ref,
                   lk2_ref, g_ref, o_ref, acc_ref, *, lambda_init):
    nk = v_ref.shape[2]
    h = pl.program_id(1)
    qi = pl.program_id(2)
    acc_ref[...] = jnp.zeros_like(acc_ref)
    c_before = tbl_ref[h, REL_BUCKETS // 2 - 1] * LOG2E
    c_after = tbl_ref[h, REL_BUCKETS - 1] * LOG2E

    def chunk(j, ms, shift, near):
        off = pl.multiple_of(j * CHUNK, CHUNK)
        k = k_ref[0, 0, pl.ds(off, CHUNK), :]
        v_t = v_ref[0, 0, j]
        out = []
        for c in range(2):
            s = jnp.dot(k, q_ref[0, 0, c], preferred_element_type=F32)
            if near:
                s = s + bias_ref[0, j - qi + 1]
            out.append(_softmax_step(s, shift, ms[c], v_t, acc_ref, c))
        return tuple(out)

    lo = jnp.maximum(qi - 1, 0)
    hi = jnp.minimum(qi + 2, nk)
    m0 = jnp.full((1, TQ_B), -jnp.inf, F32)
    ms = (m0, m0)
    ms = lax.fori_loop(0, lo, lambda j, ms: chunk(j, ms, c_before, False), ms)
    ms = lax.fori_loop(lo, hi, lambda j, ms: chunk(j, ms, 0.0, True), ms)
    ms = lax.fori_loop(hi, nk, lambda j, ms: chunk(j, ms, c_after, False), ms)

    lam = (jnp.exp(jnp.sum(lq1_ref[...] * lk1_ref[...], keepdims=True))
           - jnp.exp(jnp.sum(lq2_ref[...] * lk2_ref[...], keepdims=True))
           + lambda_init)
    a0, a1 = acc_ref[0], acc_ref[1]
    o = (a0[0:B_V_DIM] / a0[B_V_DIM:B_V_DIM + 1]
         - lam * (a1[0:B_V_DIM] / a1[B_V_DIM:B_V_DIM + 1]))
    msq = jnp.mean(o * o, axis=0, keepdims=True)
    o_ref[0, 0] = (o * lax.rsqrt(msq + EPS) * g_ref[...]).astype(BF16)


def _attn_b(tbl, qb, kb, vb, bias, lq1, lk1, lq2, lk2, gsub, lambda_init):
    bsz, _, _, _, seq = qb.shape
    ns = seq // CHUNK
    vec = pl.BlockSpec((1, HEAD_DIM), lambda b, h, q: (0, 0))
    return pl.pallas_call(
        functools.partial(_attn_b_kernel, lambda_init=lambda_init),
        grid=(bsz, B_HEADS, seq // TQ_B),
        in_specs=[
            pl.BlockSpec(memory_space=pltpu.SMEM),
            pl.BlockSpec((1, 1, 2, 2 * HEAD_DIM, TQ_B), lambda b, h, q: (b, h, 0, 0, q)),
            pl.BlockSpec((1, 1, seq, 2 * HEAD_DIM), lambda b, h, q: (b, h, 0, 0)),
            pl.BlockSpec((1, 1, ns, B_V_DIM + ONES_ROWS, CHUNK),
                         lambda b, h, q: (b, h, 0, 0, 0)),
            pl.BlockSpec((1, 3, CHUNK, TQ_B), lambda b, h, q: (h, 0, 0, 0)),
            vec, vec, vec, vec,
            pl.BlockSpec((B_V_DIM, TQ_B), lambda b, h, q: (0, 0)),
        ],
        out_specs=pl.BlockSpec((1, 1, B_V_DIM, TQ_B), lambda b, h, q: (b, h, 0, q)),
        out_shape=jax.ShapeDtypeStruct((bsz, B_HEADS, B_V_DIM, seq), BF16),
        scratch_shapes=[pltpu.VMEM((2, B_V_DIM + ONES_ROWS, TQ_B), F32)],
        compiler_params=pltpu.CompilerParams(
            dimension_semantics=("arbitrary", "arbitrary", "arbitrary"),
            vmem_limit_bytes=VMEM_LIMIT),
        name="attn_b",
    )(tbl, qb, kb, vb, bias, lq1, lk1, lq2, lk2, gsub)


def _out_mlp_kernel(x_ref, oa_ref, ob_ref, wo_ref, g_ref, wu_ref, wd_ref, y_ref):
    tn = (((0,), (0,)), ((), ()))
    attn = lax.dot_general(oa_ref[0], wo_ref[0:A_Q, :], tn, preferred_element_type=F32)
    attn = attn + lax.dot_general(ob_ref[0], wo_ref[A_Q:A_Q + B_V, :], tn,
                                  preferred_element_type=F32)
    x1 = x_ref[0] + attn
    ms = jnp.mean(x1 * x1, axis=-1, keepdims=True)
    h = (x1 * lax.rsqrt(ms + EPS) * g_ref[...]).astype(BF16)
    acc = x1
    for c in range(wu_ref.shape[1] // MLP_FCHUNK):
        cols = slice(c * MLP_FCHUNK, (c + 1) * MLP_FCHUNK)
        u = jnp.dot(h, wu_ref[:, cols], preferred_element_type=F32)
        u = jnp.square(jnp.maximum(u, 0.0)).astype(BF16)
        acc = acc + jnp.dot(u, wd_ref[cols, :], preferred_element_type=F32)
    y_ref[0] = acc


def _out_mlp(x, oa, ob, w_out, g2, w_up, w_down):
    bsz, seq, d = x.shape
    d_ff = w_up.shape[1]
    resident = lambda shape: pl.BlockSpec(shape, lambda b, s: (0,) * len(shape),
                                          pipeline_mode=pl.Buffered(1))
    return pl.pallas_call(
        _out_mlp_kernel,
        grid=(bsz, seq // MLP_ROWS),
        in_specs=[
            pl.BlockSpec((1, MLP_ROWS, d), lambda b, s: (b, s, 0)),
            pl.BlockSpec((1, A_Q, MLP_ROWS), lambda b, s: (b, 0, s)),
            pl.BlockSpec((1, B_V, MLP_ROWS), lambda b, s: (b, 0, s)),
            resident((A_Q + B_V, d)),
            resident((1, d)),
            resident((d, d_ff)),
            resident((d_ff, d)),
        ],
        out_specs=pl.BlockSpec((1, MLP_ROWS, d), lambda b, s: (b, s, 0)),
        out_shape=jax.ShapeDtypeStruct((bsz, seq, d), F32),
        compiler_params=pltpu.CompilerParams(
            dimension_semantics=("arbitrary", "arbitrary"),
            vmem_limit_bytes=VMEM_LIMIT),
        name="out_mlp",
    )(x, oa, ob, w_out, g2, w_up, w_down)


def _rope_tables_t(seq):
    pos = jnp.arange(seq)
    row = (pos // GRID_W).astype(F32)
    col = (pos % GRID_W).astype(F32)
    half = HEAD_DIM // 2
    inv_freq = 1.0 / (ROPE_THETA ** (jnp.arange(0, half, 2, dtype=F32) / half))
    ang_r = inv_freq[:, None] * row[None, :]
    ang_c = inv_freq[:, None] * col[None, :]
    return jnp.cos(ang_r), jnp.sin(ang_r), jnp.cos(ang_c), jnp.sin(ang_c)


def kernel(x, attn_norm_g, w_in, a_q_norm_g, a_k_norm_g, b_q_norm_g, b_k_norm_g,
           lambda_q1, lambda_k1, lambda_q2, lambda_k2, b_subln_g, rel_bias,
           w_out, mlp_norm_g, w_up, w_down):
    bsz, seq, d = x.shape
    depth = w_in.shape[0]
    assert seq % CHUNK == 0 and seq % TQ_A == 0 and seq % MLP_ROWS == 0
    assert w_in.shape[2] == IN_WIDTH and w_up.shape[2] % MLP_FCHUNK == 0
    cr, sr, cc, sc = _rope_tables_t(seq)
    qscale = HEAD_DIM ** -0.5 * LOG2E
    lanes = lambda v, n: jnp.broadcast_to(v.astype(F32)[:, None], (v.shape[0], n))
    bias = _bias_tiles(rel_bias.astype(F32).T)
    tbl = rel_bias.astype(F32).T
    for l in range(depth):
        lambda_init = 0.8 - 0.6 * math.exp(-0.3 * l)
        qa, ka, va, qb, kb, vb = _in_proj(
            x, attn_norm_g[l].astype(F32)[None, :], w_in[l].T.astype(BF16),
            lanes(a_q_norm_g[l] * qscale, CHUNK), lanes(a_k_norm_g[l], CHUNK),
            lanes(b_q_norm_g[l] * qscale, CHUNK), lanes(b_k_norm_g[l], CHUNK),
            cr, sr, cc, sc)
        oa = _attn_a(qa, ka, va)
        ob = _attn_b(tbl, qb, kb, vb, bias,
                     lambda_q1[l].astype(F32)[None, :], lambda_k1[l].astype(F32)[None, :],
                     lambda_q2[l].astype(F32)[None, :], lambda_k2[l].astype(F32)[None, :],
                     lanes(b_subln_g[l] * (1.0 - lambda_init), TQ_B), lambda_init)
        x = _out_mlp(x, oa.reshape(bsz, A_Q, seq), ob.reshape(bsz, B_V, seq),
                     w_out[l].astype(BF16), mlp_norm_g[l].astype(F32)[None, :],
                     w_up[l].astype(BF16), w_down[l].astype(BF16))
    return x
```

```python
import functools
import math

import jax
import jax.numpy as jnp
from jax import lax
from jax.experimental import pallas as pl
from jax.experimental.pallas import tpu as pltpu

HEAD_DIM = 64
A_Q_HEADS = 8
A_KV_HEADS = 2
A_GROUP = A_Q_HEADS // A_KV_HEADS
B_HEADS = 4
B_V_DIM = 2 * HEAD_DIM
A_Q = A_Q_HEADS * HEAD_DIM
A_KV = A_KV_HEADS * HEAD_DIM
B_QK = B_HEADS * 2 * HEAD_DIM
B_V = B_HEADS * B_V_DIM
GRID_W = 64
ROPE_THETA = 10000.0
REL_BUCKETS = 32
EPS = 1e-6
LOG2E = 1.4426950408889634
T5_THRESHOLDS = (12, 16, 23, 32, 46, 64, 91)

OFF_QA = 0
OFF_KA = OFF_QA + A_Q
OFF_VA = OFF_KA + A_KV
OFF_QB = OFF_VA + A_KV
OFF_KB = OFF_QB + B_QK
OFF_VB = OFF_KB + B_QK
IN_WIDTH = OFF_VB + B_V

V7X_LANES = 128
BF16_SUBLANES = 16
ONES_ROWS = BF16_SUBLANES
CHUNK = 512
TQ_A = 1024
TQ_B = CHUNK
BIAS_KINDS = 5
MLP_ROWS = 512
MLP_FCHUNK = 1024
VMEM_LIMIT = 48 * 1024 * 1024
REF_KEYS = 128
FAST_UNROLL = 4
SUM_MIN = 2.0 ** -64
SUM_MAX = 2.0 ** 64

BF16 = jnp.bfloat16
F32 = jnp.float32


def _bias_tile_kernel(tbl_ref, out_ref):
    h = pl.program_id(0)
    d = pl.program_id(1) - BIAS_KINDS // 2
    kk = lax.broadcasted_iota(jnp.int32, (CHUNK, CHUNK), 0)
    qq = lax.broadcasted_iota(jnp.int32, (CHUNK, CHUNK), 1)
    rel = d * CHUNK + kk - qq
    n = jnp.abs(rel)
    large = jnp.full_like(n, REL_BUCKETS // 4)
    for t in T5_THRESHOLDS:
        large = large + (n >= t).astype(jnp.int32)
    bucket = jnp.where(rel > 0, REL_BUCKETS // 2, 0) + jnp.where(n < REL_BUCKETS // 4, n, large)
    acc = jnp.zeros((CHUNK, CHUNK), F32)
    for b in range(REL_BUCKETS):
        acc = jnp.where(bucket == b, tbl_ref[h, b] * LOG2E, acc)
    out_ref[0, 0] = acc


def _bias_tiles(rel_bias_t):
    return pl.pallas_call(
        _bias_tile_kernel,
        grid=(B_HEADS, BIAS_KINDS),
        in_specs=[pl.BlockSpec(memory_space=pltpu.SMEM)],
        out_specs=pl.BlockSpec((1, 1, CHUNK, CHUNK), lambda h, d: (h, d, 0, 0)),
        out_shape=jax.ShapeDtypeStruct((B_HEADS, BIAS_KINDS, CHUNK, CHUNK), F32),
        name="bias_tiles",
    )(rel_bias_t)


def _in_proj_kernel(x_ref, g_ref, w_ref, gqa_ref, gka_ref, gqb_ref, gkb_ref,
                    cr_ref, sr_ref, cc_ref, sc_ref,
                    qa_ref, ka_ref, va_ref, qb_ref, kb_ref, vb_ref, pt_ref):
    x = x_ref[0]
    ms = jnp.mean(x * x, axis=-1, keepdims=True)
    h = (x * lax.rsqrt(ms + EPS) * g_ref[...]).astype(BF16)
    pt_ref[...] = lax.dot_general(w_ref[...], h, (((1,), (1,)), ((), ())),
                                  preferred_element_type=F32)
    cr, sr, cc, sc = cr_ref[...], sr_ref[...], cc_ref[...], sc_ref[...]
    quarter = HEAD_DIM // 4

    def norm(r0, gain_ref):
        t = pt_ref[r0:r0 + HEAD_DIM, :]
        m = jnp.mean(t * t, axis=0, keepdims=True)
        return t * lax.rsqrt(m + EPS) * gain_ref[...]

    def rope(y):
        x1r, x2r = y[0:quarter], y[quarter:2 * quarter]
        x1c, x2c = y[2 * quarter:3 * quarter], y[3 * quarter:]
        return jnp.concatenate(
            [x1r * cr - x2r * sr, x2r * cr + x1r * sr,
             x1c * cc - x2c * sc, x2c * cc + x1c * sc], axis=0)

    zeros = jnp.zeros((HEAD_DIM, CHUNK), BF16)
    ones = jnp.ones((ONES_ROWS, CHUNK), BF16)

    for hh in range(A_Q_HEADS):
        q = rope(norm(OFF_QA + hh * HEAD_DIM, gqa_ref)).astype(BF16)
        slot = hh // A_GROUP
        qa_ref[0, hh, slot * HEAD_DIM:(slot + 1) * HEAD_DIM, :] = q
        qa_ref[0, hh, (1 - slot) * HEAD_DIM:(2 - slot) * HEAD_DIM, :] = zeros
    kt = jnp.concatenate([rope(norm(OFF_KA + kv * HEAD_DIM, gka_ref))
                          for kv in range(A_KV_HEADS)], axis=0)
    ka_ref[0] = kt.T.astype(BF16)
    for kv in range(A_KV_HEADS):
        r0 = OFF_VA + kv * HEAD_DIM
        va_ref[0, kv, 0, 0:HEAD_DIM, :] = pt_ref[r0:r0 + HEAD_DIM, :].astype(BF16)
        va_ref[0, kv, 0, HEAD_DIM:HEAD_DIM + ONES_ROWS, :] = ones

    for hb in range(B_HEADS):
        for c in range(2):
            q = norm(OFF_QB + (hb * 2 + c) * HEAD_DIM, gqb_ref).astype(BF16)
            qb_ref[0, hb, c, c * HEAD_DIM:(c + 1) * HEAD_DIM, :] = q
            qb_ref[0, hb, c, (1 - c) * HEAD_DIM:(2 - c) * HEAD_DIM, :] = zeros
        kt = jnp.concatenate([norm(OFF_KB + (hb * 2 + c) * HEAD_DIM, gkb_ref)
                              for c in range(2)], axis=0)
        kb_ref[0, hb] = kt.T.astype(BF16)
        r0 = OFF_VB + hb * B_V_DIM
        vb_ref[0, hb, 0, 0:B_V_DIM, :] = pt_ref[r0:r0 + B_V_DIM, :].astype(BF16)
        vb_ref[0, hb, 0, B_V_DIM:B_V_DIM + ONES_ROWS, :] = ones


def _in_proj(x, g, w_t, gqa, gka, gqb, gkb, cr, sr, cc, sc):
    bsz, seq, d = x.shape
    ns = seq // CHUNK
    const = lambda shape: pl.BlockSpec(shape, lambda b, s: (0,) * len(shape))
    tab = pl.BlockSpec((HEAD_DIM // 4, CHUNK), lambda b, s: (0, s))
    out_shape = (
        jax.ShapeDtypeStruct((bsz, A_Q_HEADS, 2 * HEAD_DIM, seq), BF16),
        jax.ShapeDtypeStruct((bsz, seq, 2 * HEAD_DIM), BF16),
        jax.ShapeDtypeStruct((bsz, A_KV_HEADS, ns, HEAD_DIM + ONES_ROWS, CHUNK), BF16),
        jax.ShapeDtypeStruct((bsz, B_HEADS, 2, 2 * HEAD_DIM, seq), BF16),
        jax.ShapeDtypeStruct((bsz, B_HEADS, seq, 2 * HEAD_DIM), BF16),
        jax.ShapeDtypeStruct((bsz, B_HEADS, ns, B_V_DIM + ONES_ROWS, CHUNK), BF16),
    )
    out_specs = (
        pl.BlockSpec((1, A_Q_HEADS, 2 * HEAD_DIM, CHUNK), lambda b, s: (b, 0, 0, s)),
        pl.BlockSpec((1, CHUNK, 2 * HEAD_DIM), lambda b, s: (b, s, 0)),
        pl.BlockSpec((1, A_KV_HEADS, 1, HEAD_DIM + ONES_ROWS, CHUNK),
                     lambda b, s: (b, 0, s, 0, 0)),
        pl.BlockSpec((1, B_HEADS, 2, 2 * HEAD_DIM, CHUNK), lambda b, s: (b, 0, 0, 0, s)),
        pl.BlockSpec((1, B_HEADS, CHUNK, 2 * HEAD_DIM), lambda b, s: (b, 0, s, 0)),
        pl.BlockSpec((1, B_HEADS, 1, B_V_DIM + ONES_ROWS, CHUNK),
                     lambda b, s: (b, 0, s, 0, 0)),
    )
    return pl.pallas_call(
        _in_proj_kernel,
        grid=(bsz, ns),
        in_specs=[
            pl.BlockSpec((1, CHUNK, d), lambda b, s: (b, s, 0)),
            const((1, d)),
            const((IN_WIDTH, d)),
            const((HEAD_DIM, CHUNK)), const((HEAD_DIM, CHUNK)),
            const((HEAD_DIM, CHUNK)), const((HEAD_DIM, CHUNK)),
            tab, tab, tab, tab,
        ],
        out_specs=out_specs,
        out_shape=out_shape,
        scratch_shapes=[pltpu.VMEM((IN_WIDTH, CHUNK), F32)],
        compiler_params=pltpu.CompilerParams(
            dimension_semantics=("arbitrary", "arbitrary"),
            vmem_limit_bytes=VMEM_LIMIT),
        name="in_proj",
    )(x, g, w_t, gqa, gka, gqb, gkb, cr, sr, cc, sc)


def _softmax_step(s, m, v_t, acc_ref, idx):
    m_new = jnp.maximum(m, jnp.max(s, axis=0, keepdims=True))
    alpha = jnp.exp2(m - m_new)
    p = jnp.exp2(s - m_new).astype(BF16)
    pv = jnp.dot(v_t, p, preferred_element_type=F32)
    acc_ref[idx] = alpha * acc_ref[idx] + pv
    return m_new


def _sums_in_range(sums):
    good = jnp.logical_and(sums >= SUM_MIN, sums <= SUM_MAX)
    return jnp.min(jnp.where(good, 1.0, 0.0)) > 0.5


def _attention_core(q_of, k_chunk, v_chunk, bias_of, ref_bias, acc_ref, s_ref, nk):
    n_maps, rows, tq = acc_ref.shape
    sum_row = rows - ONES_ROWS

    def produce(j, slot):
        k = k_chunk(j)
        for c in range(n_maps):
            s_ref[slot, c] = jnp.dot(k, q_of(c), preferred_element_type=F32)

    def biased(s, j):
        return s if bias_of is None else s + bias_of(j)

    k_head = k_chunk(0)[0:REF_KEYS]
    refs = [jnp.max(jnp.dot(k_head, q_of(c), preferred_element_type=F32),
                    axis=0, keepdims=True) + ref_bias for c in range(n_maps)]
    acc_ref[...] = jnp.zeros_like(acc_ref)

    produce(0, 0)

    def fast_pair(i, carry):
        for slot in range(2):
            j = 2 * i + slot
            produce(jnp.minimum(j + 1, nk - 1), 1 - slot)
            v_t = v_chunk(j)
            for c in range(n_maps):
                p = jnp.exp2(biased(s_ref[slot, c], j) - refs[c]).astype(BF16)
                acc_ref[c] += jnp.dot(v_t, p, preferred_element_type=F32)
        return carry

    lax.fori_loop(0, nk // 2, fast_pair, 0, unroll=FAST_UNROLL)

    @pl.when(jnp.logical_not(_sums_in_range(acc_ref[:, sum_row:sum_row + 1, :])))
    def _():
        acc_ref[...] = jnp.zeros_like(acc_ref)

        def step(j, ms):
            k, v_t = k_chunk(j), v_chunk(j)
            return tuple(
                _softmax_step(biased(jnp.dot(k, q_of(c), preferred_element_type=F32), j),
                              ms[c], v_t, acc_ref, c)
                for c in range(n_maps))

        lax.fori_loop(0, nk, step, (jnp.full((1, tq), -jnp.inf, F32),) * n_maps)


def _kv_rows(j):
    return pl.ds(pl.multiple_of(j * CHUNK, CHUNK), CHUNK)


def _attn_a_kernel(q_ref, k_ref, v_ref, o_ref, acc_ref, s_ref):
    _attention_core(
        q_of=lambda c: q_ref[0, 0],
        k_chunk=lambda j: k_ref[0, _kv_rows(j), :],
        v_chunk=lambda j: v_ref[0, 0, j],
        bias_of=None, ref_bias=0.0, acc_ref=acc_ref, s_ref=s_ref, nk=v_ref.shape[2])
    acc = acc_ref[0]
    o_ref[0, 0] = (acc[0:HEAD_DIM] / acc[HEAD_DIM:HEAD_DIM + 1]).astype(BF16)


def _attn_a(qa, ka, va):
    bsz, _, _, seq = qa.shape
    ns = seq // CHUNK
    return pl.pallas_call(
        _attn_a_kernel,
        grid=(bsz, A_Q_HEADS, seq // TQ_A),
        in_specs=[
            pl.BlockSpec((1, 1, 2 * HEAD_DIM, TQ_A), lambda b, h, q: (b, h, 0, q)),
            pl.BlockSpec((1, seq, 2 * HEAD_DIM), lambda b, h, q: (b, 0, 0)),
            pl.BlockSpec((1, 1, ns, HEAD_DIM + ONES_ROWS, CHUNK),
                         lambda b, h, q: (b, h // A_GROUP, 0, 0, 0)),
        ],
        out_specs=pl.BlockSpec((1, 1, HEAD_DIM, TQ_A), lambda b, h, q: (b, h, 0, q)),
        out_shape=jax.ShapeDtypeStruct((bsz, A_Q_HEADS, HEAD_DIM, seq), BF16),
        scratch_shapes=[pltpu.VMEM((1, HEAD_DIM + ONES_ROWS, TQ_A), F32),
                        pltpu.VMEM((2, 1, CHUNK, TQ_A), F32)],
        compiler_params=pltpu.CompilerParams(
            dimension_semantics=("arbitrary", "arbitrary", "arbitrary"),
            vmem_limit_bytes=VMEM_LIMIT),
        name="attn_a",
    )(qa, ka, va)


def _attn_b_kernel(tbl_ref, q_ref, k_ref, v_ref, bias_ref, lq1_ref, lk1_ref, lq2_ref,
                   lk2_ref, g_ref, o_ref, acc_ref, s_ref, *, lambda_init):
    h = pl.program_id(1)
    qi = pl.program_id(2)
    bias_max = tbl_ref[h, 0]
    for b in range(1, REL_BUCKETS):
        bias_max = jnp.maximum(bias_max, tbl_ref[h, b])
    far = BIAS_KINDS // 2
    _attention_core(
        q_of=lambda c: q_ref[0, 0, c],
        k_chunk=lambda j: k_ref[0, 0, _kv_rows(j), :],
        v_chunk=lambda j: v_ref[0, 0, j],
        bias_of=lambda j: bias_ref[0, jnp.clip(j - qi, -far, far) + far],
        ref_bias=bias_max * LOG2E, acc_ref=acc_ref, s_ref=s_ref, nk=v_ref.shape[2])

    lam = (jnp.exp(jnp.sum(lq1_ref[...] * lk1_ref[...], keepdims=True))
           - jnp.exp(jnp.sum(lq2_ref[...] * lk2_ref[...], keepdims=True))
           + lambda_init)
    a0, a1 = acc_ref[0], acc_ref[1]
    o = (a0[0:B_V_DIM] / a0[B_V_DIM:B_V_DIM + 1]
         - lam * (a1[0:B_V_DIM] / a1[B_V_DIM:B_V_DIM + 1]))
    msq = jnp.mean(o * o, axis=0, keepdims=True)
    o_ref[0, 0] = (o * lax.rsqrt(msq + EPS) * g_ref[...]).astype(BF16)


def _attn_b(tbl, qb, kb, vb, bias, lq1, lk1, lq2, lk2, gsub, lambda_init):
    bsz, _, _, _, seq = qb.shape
    ns = seq // CHUNK
    vec = pl.BlockSpec((1, HEAD_DIM), lambda b, h, q: (0, 0))
    return pl.pallas_call(
        functools.partial(_attn_b_kernel, lambda_init=lambda_init),
        grid=(bsz, B_HEADS, seq // TQ_B),
        in_specs=[
            pl.BlockSpec(memory_space=pltpu.SMEM),
            pl.BlockSpec((1, 1, 2, 2 * HEAD_DIM, TQ_B), lambda b, h, q: (b, h, 0, 0, q)),
            pl.BlockSpec((1, 1, seq, 2 * HEAD_DIM), lambda b, h, q: (b, h, 0, 0)),
            pl.BlockSpec((1, 1, ns, B_V_DIM + ONES_ROWS, CHUNK),
                         lambda b, h, q: (b, h, 0, 0, 0)),
            pl.BlockSpec((1, BIAS_KINDS, CHUNK, TQ_B), lambda b, h, q: (h, 0, 0, 0)),
            vec, vec, vec, vec,
            pl.BlockSpec((B_V_DIM, TQ_B), lambda b, h, q: (0, 0)),
        ],
        out_specs=pl.BlockSpec((1, 1, B_V_DIM, TQ_B), lambda b, h, q: (b, h, 0, q)),
        out_shape=jax.ShapeDtypeStruct((bsz, B_HEADS, B_V_DIM, seq), BF16),
        scratch_shapes=[pltpu.VMEM((2, B_V_DIM + ONES_ROWS, TQ_B), F32),
                        pltpu.VMEM((2, 2, CHUNK, TQ_B), F32)],
        compiler_params=pltpu.CompilerParams(
            dimension_semantics=("arbitrary", "arbitrary", "arbitrary"),
            vmem_limit_bytes=VMEM_LIMIT),
        name="attn_b",
    )(tbl, qb, kb, vb, bias, lq1, lk1, lq2, lk2, gsub)


def _out_mlp_kernel(x_ref, oa_ref, ob_ref, wo_ref, g_ref, wu_ref, wd_ref, y_ref):
    tn = (((0,), (0,)), ((), ()))
    attn = lax.dot_general(oa_ref[0], wo_ref[0:A_Q, :], tn, preferred_element_type=F32)
    attn = attn + lax.dot_general(ob_ref[0], wo_ref[A_Q:A_Q + B_V, :], tn,
                                  preferred_element_type=F32)
    x1 = x_ref[0] + attn
    ms = jnp.mean(x1 * x1, axis=-1, keepdims=True)
    h = (x1 * lax.rsqrt(ms + EPS) * g_ref[...]).astype(BF16)
    acc = x1
    for c in range(wu_ref.shape[1] // MLP_FCHUNK):
        cols = slice(c * MLP_FCHUNK, (c + 1) * MLP_FCHUNK)
        u = jnp.dot(h, wu_ref[:, cols], preferred_element_type=F32)
        u = jnp.square(jnp.maximum(u, 0.0)).astype(BF16)
        acc = acc + jnp.dot(u, wd_ref[cols, :], preferred_element_type=F32)
    y_ref[0] = acc


def _out_mlp(x, oa, ob, w_out, g2, w_up, w_down):
    bsz, seq, d = x.shape
    d_ff = w_up.shape[1]
    resident = lambda shape: pl.BlockSpec(shape, lambda b, s: (0,) * len(shape),
                                          pipeline_mode=pl.Buffered(1))
    return pl.pallas_call(
        _out_mlp_kernel,
        grid=(bsz, seq // MLP_ROWS),
        in_specs=[
            pl.BlockSpec((1, MLP_ROWS, d), lambda b, s: (b, s, 0)),
            pl.BlockSpec((1, A_Q, MLP_ROWS), lambda b, s: (b, 0, s)),
            pl.BlockSpec((1, B_V, MLP_ROWS), lambda b, s: (b, 0, s)),
            resident((A_Q + B_V, d)),
            resident((1, d)),
            resident((d, d_ff)),
            resident((d_ff, d)),
        ],
        out_specs=pl.BlockSpec((1, MLP_ROWS, d), lambda b, s: (b, s, 0)),
        out_shape=jax.ShapeDtypeStruct((bsz, seq, d), F32),
        compiler_params=pltpu.CompilerParams(
            dimension_semantics=("arbitrary", "arbitrary"),
            vmem_limit_bytes=VMEM_LIMIT),
        name="out_mlp",
    )(x, oa, ob, w_out, g2, w_up, w_down)


def _rope_tables_t(seq):
    pos = jnp.arange(seq)
    row = (pos // GRID_W).astype(F32)
    col = (pos % GRID_W).astype(F32)
    half = HEAD_DIM // 2
    inv_freq = 1.0 / (ROPE_THETA ** (jnp.arange(0, half, 2, dtype=F32) / half))
    ang_r = inv_freq[:, None] * row[None, :]
    ang_c = inv_freq[:, None] * col[None, :]
    return jnp.cos(ang_r), jnp.sin(ang_r), jnp.cos(ang_c), jnp.sin(ang_c)


def kernel(x, attn_norm_g, w_in, a_q_norm_g, a_k_norm_g, b_q_norm_g, b_k_norm_g,
           lambda_q1, lambda_k1, lambda_q2, lambda_k2, b_subln_g, rel_bias,
           w_out, mlp_norm_g, w_up, w_down):
    bsz, seq, d = x.shape
    depth = w_in.shape[0]
    assert seq % CHUNK == 0 and seq % TQ_A == 0 and seq % MLP_ROWS == 0
    assert w_in.shape[2] == IN_WIDTH and w_up.shape[2] % MLP_FCHUNK == 0
    cr, sr, cc, sc = _rope_tables_t(seq)
    qscale = HEAD_DIM ** -0.5 * LOG2E
    lanes = lambda v, n: jnp.broadcast_to(v.astype(F32)[:, None], (v.shape[0], n))
    bias = _bias_tiles(rel_bias.astype(F32).T)
    tbl = rel_bias.astype(F32).T
    for l in range(depth):
        lambda_init = 0.8 - 0.6 * math.exp(-0.3 * l)
        qa, ka, va, qb, kb, vb = _in_proj(
            x, attn_norm_g[l].astype(F32)[None, :], w_in[l].T.astype(BF16),
            lanes(a_q_norm_g[l] * qscale, CHUNK), lanes(a_k_norm_g[l], CHUNK),
            lanes(b_q_norm_g[l] * qscale, CHUNK), lanes(b_k_norm_g[l], CHUNK),
            cr, sr, cc, sc)
        oa = _attn_a(qa, ka, va)
        ob = _attn_b(tbl, qb, kb, vb, bias,
                     lambda_q1[l].astype(F32)[None, :], lambda_k1[l].astype(F32)[None, :],
                     lambda_q2[l].astype(F32)[None, :], lambda_k2[l].astype(F32)[None, :],
                     lanes(b_subln_g[l] * (1.0 - lambda_init), TQ_B), lambda_init)
        x = _out_mlp(x, oa.reshape(bsz, A_Q, seq), ob.reshape(bsz, B_V, seq),
                     w_out[l].astype(BF16), mlp_norm_g[l].astype(F32)[None, :],
                     w_up[l].astype(BF16), w_down[l].astype(BF16))
    return x
```

```python
import functools
import math

import jax
import jax.numpy as jnp
from jax import lax
from jax.experimental import pallas as pl
from jax.experimental.pallas import tpu as pltpu

HEAD_DIM = 64
A_Q_HEADS = 8
A_KV_HEADS = 2
A_GROUP = A_Q_HEADS // A_KV_HEADS
B_HEADS = 4
B_V_DIM = 2 * HEAD_DIM
A_Q = A_Q_HEADS * HEAD_DIM
A_KV = A_KV_HEADS * HEAD_DIM
B_QK = B_HEADS * 2 * HEAD_DIM
B_V = B_HEADS * B_V_DIM
GRID_W = 64
ROPE_THETA = 10000.0
REL_BUCKETS = 32
EPS = 1e-6
LOG2E = 1.4426950408889634
T5_THRESHOLDS = (12, 16, 23, 32, 46, 64, 91)

OFF_QA = 0
OFF_KA = OFF_QA + A_Q
OFF_VA = OFF_KA + A_KV
OFF_QB = OFF_VA + A_KV
OFF_KB = OFF_QB + B_QK
OFF_VB = OFF_KB + B_QK
IN_WIDTH = OFF_VB + B_V

V7X_LANES = 128
BF16_SUBLANES = 16
ONES_ROWS = BF16_SUBLANES
CHUNK = 512
TQ_A = 1024
TQ_B = CHUNK
BIAS_KINDS = 5
MLP_ROWS = 512
MLP_FCHUNK = 1024
VMEM_LIMIT = 48 * 1024 * 1024
REF_KEYS = 128
FAST_UNROLL = 4
SUM_MIN = 2.0 ** -64
SUM_MAX = 2.0 ** 64

BF16 = jnp.bfloat16
F32 = jnp.float32


def _bias_tile_kernel(tbl_ref, out_ref):
    h = pl.program_id(0)
    d = pl.program_id(1) - BIAS_KINDS // 2
    kk = lax.broadcasted_iota(jnp.int32, (CHUNK, CHUNK), 0)
    qq = lax.broadcasted_iota(jnp.int32, (CHUNK, CHUNK), 1)
    rel = d * CHUNK + kk - qq
    n = jnp.abs(rel)
    large = jnp.full_like(n, REL_BUCKETS // 4)
    for t in T5_THRESHOLDS:
        large = large + (n >= t).astype(jnp.int32)
    bucket = jnp.where(rel > 0, REL_BUCKETS // 2, 0) + jnp.where(n < REL_BUCKETS // 4, n, large)
    acc = jnp.zeros((CHUNK, CHUNK), F32)
    for b in range(REL_BUCKETS):
        acc = jnp.where(bucket == b, tbl_ref[h, b] * LOG2E, acc)
    out_ref[0, 0] = acc


def _bias_tiles(rel_bias_t):
    return pl.pallas_call(
        _bias_tile_kernel,
        grid=(B_HEADS, BIAS_KINDS),
        in_specs=[pl.BlockSpec(memory_space=pltpu.SMEM)],
        out_specs=pl.BlockSpec((1, 1, CHUNK, CHUNK), lambda h, d: (h, d, 0, 0)),
        out_shape=jax.ShapeDtypeStruct((B_HEADS, BIAS_KINDS, CHUNK, CHUNK), F32),
        name="bias_tiles",
    )(rel_bias_t)


def _in_proj_kernel(x_ref, g_ref, w_ref, gqa_ref, gka_ref, gqb_ref, gkb_ref,
                    cr_ref, sr_ref, cc_ref, sc_ref,
                    qa_ref, ka_ref, va_ref, qb_ref, kb_ref, vb_ref, pt_ref):
    x = x_ref[0]
    ms = jnp.mean(x * x, axis=-1, keepdims=True)
    h = (x * lax.rsqrt(ms + EPS) * g_ref[...]).astype(BF16)
    pt_ref[...] = lax.dot_general(w_ref[...], h, (((1,), (1,)), ((), ())),
                                  preferred_element_type=F32)
    cr, sr, cc, sc = cr_ref[...], sr_ref[...], cc_ref[...], sc_ref[...]
    quarter = HEAD_DIM // 4

    def norm(r0, gain_ref):
        t = pt_ref[r0:r0 + HEAD_DIM, :]
        m = jnp.mean(t * t, axis=0, keepdims=True)
        return t * lax.rsqrt(m + EPS) * gain_ref[...]

    def rope(y):
        x1r, x2r = y[0:quarter], y[quarter:2 * quarter]
        x1c, x2c = y[2 * quarter:3 * quarter], y[3 * quarter:]
        return jnp.concatenate(
            [x1r * cr - x2r * sr, x2r * cr + x1r * sr,
             x1c * cc - x2c * sc, x2c * cc + x1c * sc], axis=0)

    zeros = jnp.zeros((HEAD_DIM, CHUNK), BF16)
    ones = jnp.ones((ONES_ROWS, CHUNK), BF16)

    for hh in range(A_Q_HEADS):
        q = rope(norm(OFF_QA + hh * HEAD_DIM, gqa_ref)).astype(BF16)
        slot = hh // A_GROUP
        qa_ref[0, hh, slot * HEAD_DIM:(slot + 1) * HEAD_DIM, :] = q
        qa_ref[0, hh, (1 - slot) * HEAD_DIM:(2 - slot) * HEAD_DIM, :] = zeros
    kt = jnp.concatenate([rope(norm(OFF_KA + kv * HEAD_DIM, gka_ref))
                          for kv in range(A_KV_HEADS)], axis=0)
    ka_ref[0] = kt.T.astype(BF16)
    for kv in range(A_KV_HEADS):
        r0 = OFF_VA + kv * HEAD_DIM
        va_ref[0, kv, 0, 0:HEAD_DIM, :] = pt_ref[r0:r0 + HEAD_DIM, :].astype(BF16)
        va_ref[0, kv, 0, HEAD_DIM:HEAD_DIM + ONES_ROWS, :] = ones

    for hb in range(B_HEADS):
        for c in range(2):
            q = norm(OFF_QB + (hb * 2 + c) * HEAD_DIM, gqb_ref).astype(BF16)
            qb_ref[0, hb, c, c * HEAD_DIM:(c + 1) * HEAD_DIM, :] = q
            qb_ref[0, hb, c, (1 - c) * HEAD_DIM:(2 - c) * HEAD_DIM, :] = zeros
        kt = jnp.concatenate([norm(OFF_KB + (hb * 2 + c) * HEAD_DIM, gkb_ref)
                              for c in range(2)], axis=0)
        kb_ref[0, hb] = kt.T.astype(BF16)
        r0 = OFF_VB + hb * B_V_DIM
        vb_ref[0, hb, 0, 0:B_V_DIM, :] = pt_ref[r0:r0 + B_V_DIM, :].astype(BF16)
        vb_ref[0, hb, 0, B_V_DIM:B_V_DIM + ONES_ROWS, :] = ones


def _in_proj(x, g, w_t, gqa, gka, gqb, gkb, cr, sr, cc, sc):
    bsz, seq, d = x.shape
    ns = seq // CHUNK
    const = lambda shape: pl.BlockSpec(shape, lambda b, s: (0,) * len(shape))
    tab = pl.BlockSpec((HEAD_DIM // 4, CHUNK), lambda b, s: (0, s))
    out_shape = (
        jax.ShapeDtypeStruct((bsz, A_Q_HEADS, 2 * HEAD_DIM, seq), BF16),
        jax.ShapeDtypeStruct((bsz, seq, 2 * HEAD_DIM), BF16),
        jax.ShapeDtypeStruct((bsz, A_KV_HEADS, ns, HEAD_DIM + ONES_ROWS, CHUNK), BF16),
        jax.ShapeDtypeStruct((bsz, B_HEADS, 2, 2 * HEAD_DIM, seq), BF16),
        jax.ShapeDtypeStruct((bsz, B_HEADS, seq, 2 * HEAD_DIM), BF16),
        jax.ShapeDtypeStruct((bsz, B_HEADS, ns, B_V_DIM + ONES_ROWS, CHUNK), BF16),
    )
    out_specs = (
        pl.BlockSpec((1, A_Q_HEADS, 2 * HEAD_DIM, CHUNK), lambda b, s: (b, 0, 0, s)),
        pl.BlockSpec((1, CHUNK, 2 * HEAD_DIM), lambda b, s: (b, s, 0)),
        pl.BlockSpec((1, A_KV_HEADS, 1, HEAD_DIM + ONES_ROWS, CHUNK),
                     lambda b, s: (b, 0, s, 0, 0)),
        pl.BlockSpec((1, B_HEADS, 2, 2 * HEAD_DIM, CHUNK), lambda b, s: (b, 0, 0, 0, s)),
        pl.BlockSpec((1, B_HEADS, CHUNK, 2 * HEAD_DIM), lambda b, s: (b, 0, s, 0)),
        pl.BlockSpec((1, B_HEADS, 1, B_V_DIM + ONES_ROWS, CHUNK),
                     lambda b, s: (b, 0, s, 0, 0)),
    )
    return pl.pallas_call(
        _in_proj_kernel,
        grid=(bsz, ns),
        in_specs=[
            pl.BlockSpec((1, CHUNK, d), lambda b, s: (b, s, 0)),
            const((1, d)),
            const((IN_WIDTH, d)),
            const((HEAD_DIM, CHUNK)), const((HEAD_DIM, CHUNK)),
            const((HEAD_DIM, CHUNK)), const((HEAD_DIM, CHUNK)),
            tab, tab, tab, tab,
        ],
        out_specs=out_specs,
        out_shape=out_shape,
        scratch_shapes=[pltpu.VMEM((IN_WIDTH, CHUNK), F32)],
        compiler_params=pltpu.CompilerParams(
            dimension_semantics=("arbitrary", "arbitrary"),
            vmem_limit_bytes=VMEM_LIMIT),
        name="in_proj",
    )(x, g, w_t, gqa, gka, gqb, gkb, cr, sr, cc, sc)


def _softmax_step(s, m, v_t, acc_ref, idx):
    m_new = jnp.maximum(m, jnp.max(s, axis=0, keepdims=True))
    alpha = jnp.exp2(m - m_new)
    p = jnp.exp2(s - m_new).astype(BF16)
    pv = jnp.dot(v_t, p, preferred_element_type=F32)
    acc_ref[idx] = alpha * acc_ref[idx] + pv
    return m_new


def _sums_in_range(sums):
    good = jnp.logical_and(sums >= SUM_MIN, sums <= SUM_MAX)
    return jnp.min(jnp.where(good, 1.0, 0.0)) > 0.5


def _attention_core(q_of, k_chunk, v_chunk, bias_of, ref_bias, acc_ref, s_ref, nk):
    n_maps, rows, tq = acc_ref.shape
    sum_row = rows - ONES_ROWS

    def biased(s, j):
        return s if bias_of is None else s + bias_of(j)

    def produce(j, slot):
        k = k_chunk(j)
        for c in range(n_maps):
            s = biased(jnp.dot(k, q_of(c), preferred_element_type=F32), j)
            s_ref[slot, c] = jnp.exp2(s - refs[c]).astype(BF16)

    k_head = k_chunk(0)[0:REF_KEYS]
    refs = [jnp.max(jnp.dot(k_head, q_of(c), preferred_element_type=F32),
                    axis=0, keepdims=True) + ref_bias for c in range(n_maps)]
    acc_ref[...] = jnp.zeros_like(acc_ref)

    produce(0, 0)

    def fast_pair(i, carry):
        for slot in range(2):
            j = 2 * i + slot
            produce(jnp.minimum(j + 1, nk - 1), 1 - slot)
            v_t = v_chunk(j)
            for c in range(n_maps):
                acc_ref[c] += jnp.dot(v_t, s_ref[slot, c], preferred_element_type=F32)
        return carry

    lax.fori_loop(0, nk // 2, fast_pair, 0, unroll=FAST_UNROLL)

    @pl.when(jnp.logical_not(_sums_in_range(acc_ref[:, sum_row:sum_row + 1, :])))
    def _():
        acc_ref[...] = jnp.zeros_like(acc_ref)

        def step(j, ms):
            k, v_t = k_chunk(j), v_chunk(j)
            return tuple(
                _softmax_step(biased(jnp.dot(k, q_of(c), preferred_element_type=F32), j),
                              ms[c], v_t, acc_ref, c)
                for c in range(n_maps))

        lax.fori_loop(0, nk, step, (jnp.full((1, tq), -jnp.inf, F32),) * n_maps)


def _kv_rows(j):
    return pl.ds(pl.multiple_of(j * CHUNK, CHUNK), CHUNK)


def _attn_a_kernel(q_ref, k_ref, v_ref, o_ref, acc_ref, s_ref):
    _attention_core(
        q_of=lambda c: q_ref[0, 0],
        k_chunk=lambda j: k_ref[0, _kv_rows(j), :],
        v_chunk=lambda j: v_ref[0, 0, j],
        bias_of=None, ref_bias=0.0, acc_ref=acc_ref, s_ref=s_ref, nk=v_ref.shape[2])
    acc = acc_ref[0]
    o_ref[0, 0] = (acc[0:HEAD_DIM] / acc[HEAD_DIM:HEAD_DIM + 1]).astype(BF16)


def _attn_a(qa, ka, va):
    bsz, _, _, seq = qa.shape
    ns = seq // CHUNK
    return pl.pallas_call(
        _attn_a_kernel,
        grid=(bsz, A_Q_HEADS, seq // TQ_A),
        in_specs=[
            pl.BlockSpec((1, 1, 2 * HEAD_DIM, TQ_A), lambda b, h, q: (b, h, 0, q)),
            pl.BlockSpec((1, seq, 2 * HEAD_DIM), lambda b, h, q: (b, 0, 0)),
            pl.BlockSpec((1, 1, ns, HEAD_DIM + ONES_ROWS, CHUNK),
                         lambda b, h, q: (b, h // A_GROUP, 0, 0, 0)),
        ],
        out_specs=pl.BlockSpec((1, 1, HEAD_DIM, TQ_A), lambda b, h, q: (b, h, 0, q)),
        out_shape=jax.ShapeDtypeStruct((bsz, A_Q_HEADS, HEAD_DIM, seq), BF16),
        scratch_shapes=[pltpu.VMEM((1, HEAD_DIM + ONES_ROWS, TQ_A), F32),
                        pltpu.VMEM((2, 1, CHUNK, TQ_A), BF16)],
        compiler_params=pltpu.CompilerParams(
            dimension_semantics=("arbitrary", "arbitrary", "arbitrary"),
            vmem_limit_bytes=VMEM_LIMIT),
        name="attn_a",
    )(qa, ka, va)


def _attn_b_kernel(tbl_ref, q_ref, k_ref, v_ref, bias_ref, lq1_ref, lk1_ref, lq2_ref,
                   lk2_ref, g_ref, o_ref, acc_ref, s_ref, *, lambda_init):
    h = pl.program_id(1)
    qi = pl.program_id(2)
    bias_max = tbl_ref[h, 0]
    for b in range(1, REL_BUCKETS):
        bias_max = jnp.maximum(bias_max, tbl_ref[h, b])
    far = BIAS_KINDS // 2
    _attention_core(
        q_of=lambda c: q_ref[0, 0, c],
        k_chunk=lambda j: k_ref[0, 0, _kv_rows(j), :],
        v_chunk=lambda j: v_ref[0, 0, j],
        bias_of=lambda j: bias_ref[0, jnp.clip(j - qi, -far, far) + far],
        ref_bias=bias_max * LOG2E, acc_ref=acc_ref, s_ref=s_ref, nk=v_ref.shape[2])

    lam = (jnp.exp(jnp.sum(lq1_ref[...] * lk1_ref[...], keepdims=True))
           - jnp.exp(jnp.sum(lq2_ref[...] * lk2_ref[...], keepdims=True))
           + lambda_init)
    a0, a1 = acc_ref[0], acc_ref[1]
    o = (a0[0:B_V_DIM] / a0[B_V_DIM:B_V_DIM + 1]
         - lam * (a1[0:B_V_DIM] / a1[B_V_DIM:B_V_DIM + 1]))
    msq = jnp.mean(o * o, axis=0, keepdims=True)
    o_ref[0, 0] = (o * lax.rsqrt(msq + EPS) * g_ref[...]).astype(BF16)


def _attn_b(tbl, qb, kb, vb, bias, lq1, lk1, lq2, lk2, gsub, lambda_init):
    bsz, _, _, _, seq = qb.shape
    ns = seq // CHUNK
    vec = pl.BlockSpec((1, HEAD_DIM), lambda b, h, q: (0, 0))
    return pl.pallas_call(
        functools.partial(_attn_b_kernel, lambda_init=lambda_init),
        grid=(bsz, B_HEADS, seq // TQ_B),
        in_specs=[
            pl.BlockSpec(memory_space=pltpu.SMEM),
            pl.BlockSpec((1, 1, 2, 2 * HEAD_DIM, TQ_B), lambda b, h, q: (b, h, 0, 0, q)),
            pl.BlockSpec((1, 1, seq, 2 * HEAD_DIM), lambda b, h, q: (b, h, 0, 0)),
            pl.BlockSpec((1, 1, ns, B_V_DIM + ONES_ROWS, CHUNK),
                         lambda b, h, q: (b, h, 0, 0, 0)),
            pl.BlockSpec((1, BIAS_KINDS, CHUNK, TQ_B), lambda b, h, q: (h, 0, 0, 0)),
            vec, vec, vec, vec,
            pl.BlockSpec((B_V_DIM, TQ_B), lambda b, h, q: (0, 0)),
        ],
        out_specs=pl.BlockSpec((1, 1, B_V_DIM, TQ_B), lambda b, h, q: (b, h, 0, q)),
        out_shape=jax.ShapeDtypeStruct((bsz, B_HEADS, B_V_DIM, seq), BF16),
        scratch_shapes=[pltpu.VMEM((2, B_V_DIM + ONES_ROWS, TQ_B), F32),
                        pltpu.VMEM((2, 2, CHUNK, TQ_B), BF16)],
        compiler_params=pltpu.CompilerParams(
            dimension_semantics=("arbitrary", "arbitrary", "arbitrary"),
            vmem_limit_bytes=VMEM_LIMIT),
        name="attn_b",
    )(tbl, qb, kb, vb, bias, lq1, lk1, lq2, lk2, gsub)


def _out_mlp_kernel(x_ref, oa_ref, ob_ref, wo_ref, g_ref, wu_ref, wd_ref, y_ref):
    tn = (((0,), (0,)), ((), ()))
    attn = lax.dot_general(oa_ref[0], wo_ref[0:A_Q, :], tn, preferred_element_type=F32)
    attn = attn + lax.dot_general(ob_ref[0], wo_ref[A_Q:A_Q + B_V, :], tn,
                                  preferred_element_type=F32)
    x1 = x_ref[0] + attn
    ms = jnp.mean(x1 * x1, axis=-1, keepdims=True)
    h = (x1 * lax.rsqrt(ms + EPS) * g_ref[...]).astype(BF16)
    acc = x1
    for c in range(wu_ref.shape[1] // MLP_FCHUNK):
        cols = slice(c * MLP_FCHUNK, (c + 1) * MLP_FCHUNK)
        u = jnp.dot(h, wu_ref[:, cols], preferred_element_type=F32)
        u = jnp.square(jnp.maximum(u, 0.0)).astype(BF16)
        acc = acc + jnp.dot(u, wd_ref[cols, :], preferred_element_type=F32)
    y_ref[0] = acc


def _out_mlp(x, oa, ob, w_out, g2, w_up, w_down):
    bsz, seq, d = x.shape
    d_ff = w_up.shape[1]
    resident = lambda shape: pl.BlockSpec(shape, lambda b, s: (0,) * len(shape),
                                          pipeline_mode=pl.Buffered(1))
    return pl.pallas_call(
        _out_mlp_kernel,
        grid=(bsz, seq // MLP_ROWS),
        in_specs=[
            pl.BlockSpec((1, MLP_ROWS, d), lambda b, s: (b, s, 0)),
            pl.BlockSpec((1, A_Q, MLP_ROWS), lambda b, s: (b, 0, s)),
            pl.BlockSpec((1, B_V, MLP_ROWS), lambda b, s: (b, 0, s)),
            resident((A_Q + B_V, d)),
            resident((1, d)),
            resident((d, d_ff)),
            resident((d_ff, d)),
        ],
        out_specs=pl.BlockSpec((1, MLP_ROWS, d), lambda b, s: (b, s, 0)),
        out_shape=jax.ShapeDtypeStruct((bsz, seq, d), F32),
        compiler_params=pltpu.CompilerParams(
            dimension_semantics=("arbitrary", "arbitrary"),
            vmem_limit_bytes=VMEM_LIMIT),
        name="out_mlp",
    )(x, oa, ob, w_out, g2, w_up, w_down)


def _rope_tables_t(seq):
    pos = jnp.arange(seq)
    row = (pos // GRID_W).astype(F32)
    col = (pos % GRID_W).astype(F32)
    half = HEAD_DIM // 2
    inv_freq = 1.0 / (ROPE_THETA ** (jnp.arange(0, half, 2, dtype=F32) / half))
    ang_r = inv_freq[:, None] * row[None, :]
    ang_c = inv_freq[:, None] * col[None, :]
    return jnp.cos(ang_r), jnp.sin(ang_r), jnp.cos(ang_c), jnp.sin(ang_c)


def kernel(x, attn_norm_g, w_in, a_q_norm_g, a_k_norm_g, b_q_norm_g, b_k_norm_g,
           lambda_q1, lambda_k1, lambda_q2, lambda_k2, b_subln_g, rel_bias,
           w_out, mlp_norm_g, w_up, w_down):
    bsz, seq, d = x.shape
    depth = w_in.shape[0]
    assert seq % CHUNK == 0 and seq % TQ_A == 0 and seq % MLP_ROWS == 0
    assert w_in.shape[2] == IN_WIDTH and w_up.shape[2] % MLP_FCHUNK == 0
    cr, sr, cc, sc = _rope_tables_t(seq)
    qscale = HEAD_DIM ** -0.5 * LOG2E
    lanes = lambda v, n: jnp.broadcast_to(v.astype(F32)[:, None], (v.shape[0], n))
    bias = _bias_tiles(rel_bias.astype(F32).T)
    tbl = rel_bias.astype(F32).T
    for l in range(depth):
        lambda_init = 0.8 - 0.6 * math.exp(-0.3 * l)
        qa, ka, va, qb, kb, vb = _in_proj(
            x, attn_norm_g[l].astype(F32)[None, :], w_in[l].T.astype(BF16),
            lanes(a_q_norm_g[l] * qscale, CHUNK), lanes(a_k_norm_g[l], CHUNK),
            lanes(b_q_norm_g[l] * qscale, CHUNK), lanes(b_k_norm_g[l], CHUNK),
            cr, sr, cc, sc)
        oa = _attn_a(qa, ka, va)
        ob = _attn_b(tbl, qb, kb, vb, bias,
                     lambda_q1[l].astype(F32)[None, :], lambda_k1[l].astype(F32)[None, :],
                     lambda_q2[l].astype(F32)[None, :], lambda_k2[l].astype(F32)[None, :],
                     lanes(b_subln_g[l] * (1.0 - lambda_init), TQ_B), lambda_init)
        x = _out_mlp(x, oa.reshape(bsz, A_Q, seq), ob.reshape(bsz, B_V, seq),
                     w_out[l].astype(BF16), mlp_norm_g[l].astype(F32)[None, :],
                     w_up[l].astype(BF16), w_down[l].astype(BF16))
    return x
```

```python
import functools
import math

import jax
import jax.numpy as jnp
from jax import lax
from jax.experimental import pallas as pl
from jax.experimental.pallas import tpu as pltpu

HEAD_DIM = 64
A_Q_HEADS = 8
A_KV_HEADS = 2
A_GROUP = A_Q_HEADS // A_KV_HEADS
B_HEADS = 4
B_V_DIM = 2 * HEAD_DIM
A_Q = A_Q_HEADS * HEAD_DIM
A_KV = A_KV_HEADS * HEAD_DIM
B_QK = B_HEADS * 2 * HEAD_DIM
B_V = B_HEADS * B_V_DIM
GRID_W = 64
ROPE_THETA = 10000.0
REL_BUCKETS = 32
EPS = 1e-6
LOG2E = 1.4426950408889634
T5_BUCKET_STARTS = tuple(range(9)) + (12, 16, 23, 32, 46, 64, 91)

OFF_QA = 0
OFF_KA = OFF_QA + A_Q
OFF_VA = OFF_KA + A_KV
OFF_QB = OFF_VA + A_KV
OFF_KB = OFF_QB + B_QK
OFF_VB = OFF_KB + B_QK
IN_WIDTH = OFF_VB + B_V

V7X_LANES = 128
BF16_SUBLANES = 16
ONES_ROWS = BF16_SUBLANES
CHUNK = 512
TQ_A = 2048
TQ_B = 1024
BIAS_KINDS = 5
MLP_ROWS = 512
MLP_FCHUNK = 1024
VMEM_LIMIT = 48 * 1024 * 1024
REF_KEYS = 128
FAST_UNROLL = 2
QBLK = CHUNK
SUM_MIN = 2.0 ** -64
SUM_MAX = 2.0 ** 64

BF16 = jnp.bfloat16
F32 = jnp.float32


def _bias_tile_kernel(tbl_ref, out_ref):
    h = pl.program_id(0)
    d = pl.program_id(1) - BIAS_KINDS // 2
    kk = lax.broadcasted_iota(jnp.int32, (CHUNK, CHUNK), 0)
    qq = lax.broadcasted_iota(jnp.int32, (CHUNK, CHUNK), 1)
    rel = d * CHUNK + kk - qq
    n = jnp.abs(rel)

    def side(first_bucket):
        v = jnp.full((CHUNK, CHUNK), tbl_ref[h, first_bucket] * LOG2E, F32)
        for b, start in enumerate(T5_BUCKET_STARTS[1:], 1):
            v = jnp.where(n >= start, tbl_ref[h, first_bucket + b] * LOG2E, v)
        return v

    @pl.when(d < 0)
    def _():
        out_ref[0, 0] = side(0)

    @pl.when(d > 0)
    def _():
        out_ref[0, 0] = side(REL_BUCKETS // 2)

    @pl.when(d == 0)
    def _():
        out_ref[0, 0] = jnp.where(rel > 0, side(REL_BUCKETS // 2), side(0))


def _bias_tiles(rel_bias_t):
    return pl.pallas_call(
        _bias_tile_kernel,
        grid=(B_HEADS, BIAS_KINDS),
        in_specs=[pl.BlockSpec(memory_space=pltpu.SMEM)],
        out_specs=pl.BlockSpec((1, 1, CHUNK, CHUNK), lambda h, d: (h, d, 0, 0)),
        out_shape=jax.ShapeDtypeStruct((B_HEADS, BIAS_KINDS, CHUNK, CHUNK), F32),
        name="bias_tiles",
    )(rel_bias_t)


def _in_proj_kernel(x_ref, g_ref, w_ref, gqa_ref, gka_ref, gqb_ref, gkb_ref,
                    cr_ref, sr_ref, cc_ref, sc_ref,
                    qa_ref, ka_ref, va_ref, qb_ref, kb_ref, vb_ref, pt_ref):
    x = x_ref[0]
    ms = jnp.mean(x * x, axis=-1, keepdims=True)
    h = (x * lax.rsqrt(ms + EPS) * g_ref[...]).astype(BF16)
    pt_ref[...] = lax.dot_general(w_ref[...], h, (((1,), (1,)), ((), ())),
                                  preferred_element_type=F32)
    cr, sr, cc, sc = cr_ref[...], sr_ref[...], cc_ref[...], sc_ref[...]
    quarter = HEAD_DIM // 4

    def norm(r0, gain_ref):
        t = pt_ref[r0:r0 + HEAD_DIM, :]
        m = jnp.mean(t * t, axis=0, keepdims=True)
        return t * lax.rsqrt(m + EPS) * gain_ref[...]

    def rope(y):
        x1r, x2r = y[0:quarter], y[quarter:2 * quarter]
        x1c, x2c = y[2 * quarter:3 * quarter], y[3 * quarter:]
        return jnp.concatenate(
            [x1r * cr - x2r * sr, x2r * cr + x1r * sr,
             x1c * cc - x2c * sc, x2c * cc + x1c * sc], axis=0)

    zeros = jnp.zeros((HEAD_DIM, CHUNK), BF16)
    ones = jnp.ones((ONES_ROWS, CHUNK), BF16)

    for hh in range(A_Q_HEADS):
        q = rope(norm(OFF_QA + hh * HEAD_DIM, gqa_ref)).astype(BF16)
        slot = hh // A_GROUP
        qa_ref[0, hh, slot * HEAD_DIM:(slot + 1) * HEAD_DIM, :] = q
        qa_ref[0, hh, (1 - slot) * HEAD_DIM:(2 - slot) * HEAD_DIM, :] = zeros
    kt = jnp.concatenate([rope(norm(OFF_KA + kv * HEAD_DIM, gka_ref))
                          for kv in range(A_KV_HEADS)], axis=0)
    ka_ref[0] = kt.T.astype(BF16)
    for kv in range(A_KV_HEADS):
        r0 = OFF_VA + kv * HEAD_DIM
        va_ref[0, kv, 0, 0:HEAD_DIM, :] = pt_ref[r0:r0 + HEAD_DIM, :].astype(BF16)
        va_ref[0, kv, 0, HEAD_DIM:HEAD_DIM + ONES_ROWS, :] = ones

    for hb in range(B_HEADS):
        for c in range(2):
            q = norm(OFF_QB + (hb * 2 + c) * HEAD_DIM, gqb_ref).astype(BF16)
            qb_ref[0, hb, c, c * HEAD_DIM:(c + 1) * HEAD_DIM, :] = q
            qb_ref[0, hb, c, (1 - c) * HEAD_DIM:(2 - c) * HEAD_DIM, :] = zeros
        kt = jnp.concatenate([norm(OFF_KB + (hb * 2 + c) * HEAD_DIM, gkb_ref)
                              for c in range(2)], axis=0)
        kb_ref[0, hb] = kt.T.astype(BF16)
        r0 = OFF_VB + hb * B_V_DIM
        vb_ref[0, hb, 0, 0:B_V_DIM, :] = pt_ref[r0:r0 + B_V_DIM, :].astype(BF16)
        vb_ref[0, hb, 0, B_V_DIM:B_V_DIM + ONES_ROWS, :] = ones


def _in_proj(x, g, w_t, gqa, gka, gqb, gkb, cr, sr, cc, sc):
    bsz, seq, d = x.shape
    ns = seq // CHUNK
    const = lambda shape: pl.BlockSpec(shape, lambda b, s: (0,) * len(shape))
    tab = pl.BlockSpec((HEAD_DIM // 4, CHUNK), lambda b, s: (0, s))
    out_shape = (
        jax.ShapeDtypeStruct((bsz, A_Q_HEADS, 2 * HEAD_DIM, seq), BF16),
        jax.ShapeDtypeStruct((bsz, seq, 2 * HEAD_DIM), BF16),
        jax.ShapeDtypeStruct((bsz, A_KV_HEADS, ns, HEAD_DIM + ONES_ROWS, CHUNK), BF16),
        jax.ShapeDtypeStruct((bsz, B_HEADS, 2, 2 * HEAD_DIM, seq), BF16),
        jax.ShapeDtypeStruct((bsz, B_HEADS, seq, 2 * HEAD_DIM), BF16),
        jax.ShapeDtypeStruct((bsz, B_HEADS, ns, B_V_DIM + ONES_ROWS, CHUNK), BF16),
    )
    out_specs = (
        pl.BlockSpec((1, A_Q_HEADS, 2 * HEAD_DIM, CHUNK), lambda b, s: (b, 0, 0, s)),
        pl.BlockSpec((1, CHUNK, 2 * HEAD_DIM), lambda b, s: (b, s, 0)),
        pl.BlockSpec((1, A_KV_HEADS, 1, HEAD_DIM + ONES_ROWS, CHUNK),
                     lambda b, s: (b, 0, s, 0, 0)),
        pl.BlockSpec((1, B_HEADS, 2, 2 * HEAD_DIM, CHUNK), lambda b, s: (b, 0, 0, 0, s)),
        pl.BlockSpec((1, B_HEADS, CHUNK, 2 * HEAD_DIM), lambda b, s: (b, 0, s, 0)),
        pl.BlockSpec((1, B_HEADS, 1, B_V_DIM + ONES_ROWS, CHUNK),
                     lambda b, s: (b, 0, s, 0, 0)),
    )
    return pl.pallas_call(
        _in_proj_kernel,
        grid=(bsz, ns),
        in_specs=[
            pl.BlockSpec((1, CHUNK, d), lambda b, s: (b, s, 0)),
            const((1, d)),
            const((IN_WIDTH, d)),
            const((HEAD_DIM, CHUNK)), const((HEAD_DIM, CHUNK)),
            const((HEAD_DIM, CHUNK)), const((HEAD_DIM, CHUNK)),
            tab, tab, tab, tab,
        ],
        out_specs=out_specs,
        out_shape=out_shape,
        scratch_shapes=[pltpu.VMEM((IN_WIDTH, CHUNK), F32)],
        compiler_params=pltpu.CompilerParams(
            dimension_semantics=("arbitrary", "arbitrary"),
            vmem_limit_bytes=VMEM_LIMIT),
        name="in_proj",
    )(x, g, w_t, gqa, gka, gqb, gkb, cr, sr, cc, sc)


def _softmax_step(s, m, v_t, acc_ref, idx):
    m_new = jnp.maximum(m, jnp.max(s, axis=0, keepdims=True))
    alpha = jnp.exp2(m - m_new)
    p = jnp.exp2(s - m_new).astype(BF16)
    pv = jnp.dot(v_t, p, preferred_element_type=F32)
    acc_ref[idx] = alpha * acc_ref[idx] + pv
    return m_new


def _sums_in_range(sums):
    good = jnp.logical_and(sums >= SUM_MIN, sums <= SUM_MAX)
    return jnp.min(jnp.where(good, 1.0, 0.0)) > 0.5


def _attention_core(q_of, k_chunk, v_chunk, bias_of, ref_bias, acc_ref, p_ref, nk, zero):
    n_maps, rows, tq = acc_ref.shape
    sum_row = rows - ONES_ROWS
    units = [(c, b0) for c in range(n_maps) for b0 in range(0, tq, QBLK)]
    cols = lambda b0: slice(b0, b0 + QBLK)

    def scores(j, c, b0):
        s = jnp.dot(k_chunk(j), q_of(c, cols(b0)), preferred_element_type=F32)
        return s if bias_of is None else s + bias_of(j, b0)

    def produce(j, slot, c, b0):
        p_ref[slot, c, :, cols(b0)] = jnp.exp2(scores(j, c, b0) - refs[c, b0]).astype(BF16)

    def consume(j, slot, c, b0):
        acc_ref[c, :, cols(b0)] += jnp.dot(v_chunk(j), p_ref[slot, c, :, cols(b0)],
                                           preferred_element_type=F32)

    k_head = k_chunk(0)[0:REF_KEYS]
    refs = {(c, b0): jnp.max(jnp.dot(k_head, q_of(c, cols(b0)), preferred_element_type=F32),
                             axis=0, keepdims=True) + ref_bias for c, b0 in units}
    acc_ref[...] = jnp.zeros_like(acc_ref)

    for c, b0 in units:
        produce(0, 0, c, b0)

    def body(ib, final):
        for t in range(2 * FAST_UNROLL):
            j = 2 * FAST_UNROLL * ib + t
            for c, b0 in units:
                if not (final and t == 2 * FAST_UNROLL - 1):
                    produce(j + 1, 1 - t % 2, c, b0)
                consume(j, t % 2, c, b0)

    n_bodies = nk // (2 * FAST_UNROLL)
    lax.fori_loop(0, zero + (n_bodies - 1), lambda ib, carry: body(ib, False) or carry, 0)
    body(n_bodies - 1, True)

    @pl.when(jnp.logical_not(_sums_in_range(acc_ref[:, sum_row:sum_row + 1, :])))
    def _():
        acc_ref[...] = jnp.zeros_like(acc_ref)

        def step(j, ms):
            v_t = v_chunk(j)
            return tuple(_softmax_step(scores(j, c, b0), m, v_t, acc_ref,
                                       (c, slice(None), cols(b0)))
                         for (c, b0), m in zip(units, ms))

        lax.fori_loop(0, nk, step, (jnp.full((1, QBLK), -jnp.inf, F32),) * len(units))


def _kv_rows(j):
    return pl.ds(pl.multiple_of(j * CHUNK, CHUNK), CHUNK)


def _traced_zero():
    return jnp.minimum(pl.program_id(0), 0)


def _attn_a_kernel(q_ref, k_ref, v_ref, o_ref, acc_ref, p_ref):
    _attention_core(
        q_of=lambda c, cols: q_ref[0, 0, :, cols],
        k_chunk=lambda j: k_ref[0, _kv_rows(j), :],
        v_chunk=lambda j: v_ref[0, 0, j],
        bias_of=None, ref_bias=0.0, acc_ref=acc_ref, p_ref=p_ref, nk=v_ref.shape[2],
        zero=_traced_zero())
    acc = acc_ref[0]
    o_ref[0, 0] = (acc[0:HEAD_DIM] / acc[HEAD_DIM:HEAD_DIM + 1]).astype(BF16)


def _attn_a(qa, ka, va):
    bsz, _, _, seq = qa.shape
    ns = seq // CHUNK
    return pl.pallas_call(
        _attn_a_kernel,
        grid=(bsz, A_Q_HEADS, seq // TQ_A),
        in_specs=[
            pl.BlockSpec((1, 1, 2 * HEAD_DIM, TQ_A), lambda b, h, q: (b, h, 0, q)),
            pl.BlockSpec((1, seq, 2 * HEAD_DIM), lambda b, h, q: (b, 0, 0)),
            pl.BlockSpec((1, 1, ns, HEAD_DIM + ONES_ROWS, CHUNK),
                         lambda b, h, q: (b, h // A_GROUP, 0, 0, 0)),
        ],
        out_specs=pl.BlockSpec((1, 1, HEAD_DIM, TQ_A), lambda b, h, q: (b, h, 0, q)),
        out_shape=jax.ShapeDtypeStruct((bsz, A_Q_HEADS, HEAD_DIM, seq), BF16),
        scratch_shapes=[pltpu.VMEM((1, HEAD_DIM + ONES_ROWS, TQ_A), F32),
                        pltpu.VMEM((2, 1, CHUNK, TQ_A), BF16)],
        compiler_params=pltpu.CompilerParams(
            dimension_semantics=("arbitrary", "arbitrary", "arbitrary"),
            vmem_limit_bytes=VMEM_LIMIT),
        name="attn_a",
    )(qa, ka, va)


def _attn_b_kernel(tbl_ref, q_ref, k_ref, v_ref, bias_ref, lq1_ref, lk1_ref, lq2_ref,
                   lk2_ref, g_ref, o_ref, acc_ref, p_ref, *, lambda_init):
    h = pl.program_id(1)
    qi = pl.program_id(2)
    bias_max = tbl_ref[h, 0]
    for b in range(1, REL_BUCKETS):
        bias_max = jnp.maximum(bias_max, tbl_ref[h, b])
    far = BIAS_KINDS // 2
    first_block = qi * (TQ_B // CHUNK)
    _attention_core(
        q_of=lambda c, cols: q_ref[0, 0, c, :, cols],
        k_chunk=lambda j: k_ref[0, 0, _kv_rows(j), :],
        v_chunk=lambda j: v_ref[0, 0, j],
        bias_of=lambda j, b0: bias_ref[
            0, jnp.clip(j - (first_block + b0 // CHUNK), -far, far) + far],
        ref_bias=bias_max * LOG2E, acc_ref=acc_ref, p_ref=p_ref, nk=v_ref.shape[2],
        zero=_traced_zero())

    lam = (jnp.exp(jnp.sum(lq1_ref[...] * lk1_ref[...], keepdims=True))
           - jnp.exp(jnp.sum(lq2_ref[...] * lk2_ref[...], keepdims=True))
           + lambda_init)
    a0, a1 = acc_ref[0], acc_ref[1]
    o = (a0[0:B_V_DIM] / a0[B_V_DIM:B_V_DIM + 1]
         - lam * (a1[0:B_V_DIM] / a1[B_V_DIM:B_V_DIM + 1]))
    msq = jnp.mean(o * o, axis=0, keepdims=True)
    o_ref[0, 0] = (o * lax.rsqrt(msq + EPS) * g_ref[...]).astype(BF16)


def _attn_b(tbl, qb, kb, vb, bias, lq1, lk1, lq2, lk2, gsub, lambda_init):
    bsz, _, _, _, seq = qb.shape
    ns = seq // CHUNK
    vec = pl.BlockSpec((1, HEAD_DIM), lambda b, h, q: (0, 0))
    return pl.pallas_call(
        functools.partial(_attn_b_kernel, lambda_init=lambda_init),
        grid=(bsz, B_HEADS, seq // TQ_B),
        in_specs=[
            pl.BlockSpec(memory_space=pltpu.SMEM),
            pl.BlockSpec((1, 1, 2, 2 * HEAD_DIM, TQ_B), lambda b, h, q: (b, h, 0, 0, q)),
            pl.BlockSpec((1, 1, seq, 2 * HEAD_DIM), lambda b, h, q: (b, h, 0, 0)),
            pl.BlockSpec((1, 1, ns, B_V_DIM + ONES_ROWS, CHUNK),
                         lambda b, h, q: (b, h, 0, 0, 0)),
            pl.BlockSpec((1, BIAS_KINDS, CHUNK, CHUNK), lambda b, h, q: (h, 0, 0, 0)),
            vec, vec, vec, vec,
            pl.BlockSpec((B_V_DIM, TQ_B), lambda b, h, q: (0, 0)),
        ],
        out_specs=pl.BlockSpec((1, 1, B_V_DIM, TQ_B), lambda b, h, q: (b, h, 0, q)),
        out_shape=jax.ShapeDtypeStruct((bsz, B_HEADS, B_V_DIM, seq), BF16),
        scratch_shapes=[pltpu.VMEM((2, B_V_DIM + ONES_ROWS, TQ_B), F32),
                        pltpu.VMEM((2, 2, CHUNK, TQ_B), BF16)],
        compiler_params=pltpu.CompilerParams(
            dimension_semantics=("arbitrary", "arbitrary", "arbitrary"),
            vmem_limit_bytes=VMEM_LIMIT),
        name="attn_b",
    )(tbl, qb, kb, vb, bias, lq1, lk1, lq2, lk2, gsub)


def _out_mlp_kernel(x_ref, oa_ref, ob_ref, wo_ref, g_ref, wu_ref, wd_ref, y_ref):
    tn = (((0,), (0,)), ((), ()))
    attn = lax.dot_general(oa_ref[0], wo_ref[0:A_Q, :], tn, preferred_element_type=F32)
    attn = attn + lax.dot_general(ob_ref[0], wo_ref[A_Q:A_Q + B_V, :], tn,
                                  preferred_element_type=F32)
    x1 = x_ref[0] + attn
    ms = jnp.mean(x1 * x1, axis=-1, keepdims=True)
    h = (x1 * lax.rsqrt(ms + EPS) * g_ref[...]).astype(BF16)
    acc = x1
    for c in range(wu_ref.shape[1] // MLP_FCHUNK):
        cols = slice(c * MLP_FCHUNK, (c + 1) * MLP_FCHUNK)
        u = jnp.dot(h, wu_ref[:, cols], preferred_element_type=F32)
        u = jnp.square(jnp.maximum(u, 0.0)).astype(BF16)
        acc = acc + jnp.dot(u, wd_ref[cols, :], preferred_element_type=F32)
    y_ref[0] = acc


def _out_mlp(x, oa, ob, w_out, g2, w_up, w_down):
    bsz, seq, d = x.shape
    d_ff = w_up.shape[1]
    resident = lambda shape: pl.BlockSpec(shape, lambda b, s: (0,) * len(shape),
                                          pipeline_mode=pl.Buffered(1))
    return pl.pallas_call(
        _out_mlp_kernel,
        grid=(bsz, seq // MLP_ROWS),
        in_specs=[
            pl.BlockSpec((1, MLP_ROWS, d), lambda b, s: (b, s, 0)),
            pl.BlockSpec((1, A_Q, MLP_ROWS), lambda b, s: (b, 0, s)),
            pl.BlockSpec((1, B_V, MLP_ROWS), lambda b, s: (b, 0, s)),
            resident((A_Q + B_V, d)),
            resident((1, d)),
            resident((d, d_ff)),
            resident((d_ff, d)),
        ],
        out_specs=pl.BlockSpec((1, MLP_ROWS, d), lambda b, s: (b, s, 0)),
        out_shape=jax.ShapeDtypeStruct((bsz, seq, d), F32),
        compiler_params=pltpu.CompilerParams(
            dimension_semantics=("arbitrary", "arbitrary"),
            vmem_limit_bytes=VMEM_LIMIT),
        name="out_mlp",
    )(x, oa, ob, w_out, g2, w_up, w_down)


def _rope_tables_t(seq):
    pos = jnp.arange(seq)
    row = (pos // GRID_W).astype(F32)
    col = (pos % GRID_W).astype(F32)
    half = HEAD_DIM // 2
    inv_freq = 1.0 / (ROPE_THETA ** (jnp.arange(0, half, 2, dtype=F32) / half))
    ang_r = inv_freq[:, None] * row[None, :]
    ang_c = inv_freq[:, None] * col[None, :]
    return jnp.cos(ang_r), jnp.sin(ang_r), jnp.cos(ang_c), jnp.sin(ang_c)


def kernel(x, attn_norm_g, w_in, a_q_norm_g, a_k_norm_g, b_q_norm_g, b_k_norm_g,
           lambda_q1, lambda_k1, lambda_q2, lambda_k2, b_subln_g, rel_bias,
           w_out, mlp_norm_g, w_up, w_down):
    bsz, seq, d = x.shape
    depth = w_in.shape[0]
    assert seq % CHUNK == 0 and seq % TQ_A == 0 and seq % MLP_ROWS == 0
    assert w_in.shape[2] == IN_WIDTH and w_up.shape[2] % MLP_FCHUNK == 0
    cr, sr, cc, sc = _rope_tables_t(seq)
    qscale = HEAD_DIM ** -0.5 * LOG2E
    lanes = lambda v, n: jnp.broadcast_to(v.astype(F32)[:, None], (v.shape[0], n))
    bias = _bias_tiles(rel_bias.astype(F32).T)
    tbl = rel_bias.astype(F32).T
    for l in range(depth):
        lambda_init = 0.8 - 0.6 * math.exp(-0.3 * l)
        qa, ka, va, qb, kb, vb = _in_proj(
            x, attn_norm_g[l].astype(F32)[None, :], w_in[l].T.astype(BF16),
            lanes(a_q_norm_g[l] * qscale, CHUNK), lanes(a_k_norm_g[l], CHUNK),
            lanes(b_q_norm_g[l] * qscale, CHUNK), lanes(b_k_norm_g[l], CHUNK),
            cr, sr, cc, sc)
        oa = _attn_a(qa, ka, va)
        ob = _attn_b(tbl, qb, kb, vb, bias,
                     lambda_q1[l].astype(F32)[None, :], lambda_k1[l].astype(F32)[None, :],
                     lambda_q2[l].astype(F32)[None, :], lambda_k2[l].astype(F32)[None, :],
                     lanes(b_subln_g[l] * (1.0 - lambda_init), TQ_B), lambda_init)
        x = _out_mlp(x, oa.reshape(bsz, A_Q, seq), ob.reshape(bsz, B_V, seq),
                     w_out[l].astype(BF16), mlp_norm_g[l].astype(F32)[None, :],
                     w_up[l].astype(BF16), w_down[l].astype(BF16))
    return x
```

```python
import functools
import math

import jax
import jax.numpy as jnp
from jax import lax
from jax.experimental import pallas as pl
from jax.experimental.pallas import tpu as pltpu

HEAD_DIM = 64
A_Q_HEADS = 8
A_KV_HEADS = 2
A_GROUP = A_Q_HEADS // A_KV_HEADS
B_HEADS = 4
B_V_DIM = 2 * HEAD_DIM
A_Q = A_Q_HEADS * HEAD_DIM
A_KV = A_KV_HEADS * HEAD_DIM
B_QK = B_HEADS * 2 * HEAD_DIM
B_V = B_HEADS * B_V_DIM
GRID_W = 64
ROPE_THETA = 10000.0
REL_BUCKETS = 32
EPS = 1e-6
LOG2E = 1.4426950408889634
T5_BUCKET_STARTS = tuple(range(9)) + (12, 16, 23, 32, 46, 64, 91)

OFF_QA = 0
OFF_KA = OFF_QA + A_Q
OFF_VA = OFF_KA + A_KV
OFF_QB = OFF_VA + A_KV
OFF_KB = OFF_QB + B_QK
OFF_VB = OFF_KB + B_QK
IN_WIDTH = OFF_VB + B_V

V7X_SUBLANES = 8
CHUNK = 512
TQ_A = 2048
TQ_B = 1024
BIAS_KINDS = 5
MLP_ROWS = 512
MLP_FCHUNK = 1024
VMEM_LIMIT = 48 * 1024 * 1024
REF_KEYS = 128
FAST_UNROLL = 2
QBLK = CHUNK
SUM_MIN = 2.0 ** -64
SUM_MAX = 2.0 ** 64

BF16 = jnp.bfloat16
F32 = jnp.float32


def _bias_tile_kernel(tbl_ref, out_ref):
    h = pl.program_id(0)
    d = pl.program_id(1) - BIAS_KINDS // 2
    kk = lax.broadcasted_iota(jnp.int32, (CHUNK, CHUNK), 0)
    qq = lax.broadcasted_iota(jnp.int32, (CHUNK, CHUNK), 1)
    rel = d * CHUNK + kk - qq
    n = jnp.abs(rel)

    def side(first_bucket):
        v = jnp.full((CHUNK, CHUNK), tbl_ref[h, first_bucket] * LOG2E, F32)
        for b, start in enumerate(T5_BUCKET_STARTS[1:], 1):
            v = jnp.where(n >= start, tbl_ref[h, first_bucket + b] * LOG2E, v)
        return v

    @pl.when(d < 0)
    def _():
        out_ref[0, 0] = side(0)

    @pl.when(d > 0)
    def _():
        out_ref[0, 0] = side(REL_BUCKETS // 2)

    @pl.when(d == 0)
    def _():
        out_ref[0, 0] = jnp.where(rel > 0, side(REL_BUCKETS // 2), side(0))


def _bias_tiles(rel_bias_t):
    return pl.pallas_call(
        _bias_tile_kernel,
        grid=(B_HEADS, BIAS_KINDS),
        in_specs=[pl.BlockSpec(memory_space=pltpu.SMEM)],
        out_specs=pl.BlockSpec((1, 1, CHUNK, CHUNK), lambda h, d: (h, d, 0, 0)),
        out_shape=jax.ShapeDtypeStruct((B_HEADS, BIAS_KINDS, CHUNK, CHUNK), F32),
        name="bias_tiles",
    )(rel_bias_t)


def _in_proj_kernel(x_ref, g_ref, w_ref, gqa_ref, gka_ref, gqb_ref, gkb_ref,
                    cr_ref, sr_ref, cc_ref, sc_ref,
                    qa_ref, ka_ref, va_ref, qb_ref, kb_ref, vb_ref, pt_ref):
    x = x_ref[0]
    ms = jnp.mean(x * x, axis=-1, keepdims=True)
    h = (x * lax.rsqrt(ms + EPS) * g_ref[...]).astype(BF16)
    cr, sr, cc, sc = cr_ref[...], sr_ref[...], cc_ref[...], sc_ref[...]
    quarter = HEAD_DIM // 4

    def project(r0, n):
        pt_ref[r0:r0 + n, :] = lax.dot_general(
            w_ref[r0:r0 + n, :], h, (((1,), (1,)), ((), ())), preferred_element_type=F32)

    def norm(r0, gain_ref):
        t = pt_ref[r0:r0 + HEAD_DIM, :]
        m = jnp.mean(t * t, axis=0, keepdims=True)
        return t * lax.rsqrt(m + EPS) * gain_ref[...]

    def rope(y):
        x1r, x2r = y[0:quarter], y[quarter:2 * quarter]
        x1c, x2c = y[2 * quarter:3 * quarter], y[3 * quarter:]
        return jnp.concatenate(
            [x1r * cr - x2r * sr, x2r * cr + x1r * sr,
             x1c * cc - x2c * sc, x2c * cc + x1c * sc], axis=0)

    zeros = jnp.zeros((HEAD_DIM, CHUNK), BF16)

    def finish_qa():
        for hh in range(A_Q_HEADS):
            q = rope(norm(OFF_QA + hh * HEAD_DIM, gqa_ref)).astype(BF16)
            slot = hh // A_GROUP
            qa_ref[0, hh, slot * HEAD_DIM:(slot + 1) * HEAD_DIM, :] = q
            qa_ref[0, hh, (1 - slot) * HEAD_DIM:(2 - slot) * HEAD_DIM, :] = zeros

    def finish_kva():
        kt = jnp.concatenate([rope(norm(OFF_KA + kv * HEAD_DIM, gka_ref))
                              for kv in range(A_KV_HEADS)], axis=0)
        ka_ref[0] = kt.T.astype(BF16)
        for kv in range(A_KV_HEADS):
            r0 = OFF_VA + kv * HEAD_DIM
            va_ref[0, kv, 0] = pt_ref[r0:r0 + HEAD_DIM, :].astype(BF16)

    def finish_qb():
        for hb in range(B_HEADS):
            for c in range(2):
                q = norm(OFF_QB + (hb * 2 + c) * HEAD_DIM, gqb_ref).astype(BF16)
                qb_ref[0, hb, c, c * HEAD_DIM:(c + 1) * HEAD_DIM, :] = q
                qb_ref[0, hb, c, (1 - c) * HEAD_DIM:(2 - c) * HEAD_DIM, :] = zeros

    def finish_kb():
        for hb in range(B_HEADS):
            kt = jnp.concatenate([norm(OFF_KB + (hb * 2 + c) * HEAD_DIM, gkb_ref)
                                  for c in range(2)], axis=0)
            kb_ref[0, hb] = kt.T.astype(BF16)

    def finish_vb():
        for hb in range(B_HEADS):
            r0 = OFF_VB + hb * B_V_DIM
            vb_ref[0, hb, 0] = pt_ref[r0:r0 + B_V_DIM, :].astype(BF16)

    project(OFF_QA, A_Q)
    project(OFF_KA, 2 * A_KV)
    finish_qa()
    project(OFF_QB, B_QK)
    finish_kva()
    project(OFF_KB, B_QK)
    finish_qb()
    project(OFF_VB, B_V)
    finish_kb()
    finish_vb()


def _in_proj(x, g, w_t, gqa, gka, gqb, gkb, cr, sr, cc, sc):
    bsz, seq, d = x.shape
    ns = seq // CHUNK
    const = lambda shape: pl.BlockSpec(shape, lambda b, s: (0,) * len(shape))
    tab = pl.BlockSpec((HEAD_DIM // 4, CHUNK), lambda b, s: (0, s))
    out_shape = (
        jax.ShapeDtypeStruct((bsz, A_Q_HEADS, 2 * HEAD_DIM, seq), BF16),
        jax.ShapeDtypeStruct((bsz, seq, 2 * HEAD_DIM), BF16),
        jax.ShapeDtypeStruct((bsz, A_KV_HEADS, ns, HEAD_DIM, CHUNK), BF16),
        jax.ShapeDtypeStruct((bsz, B_HEADS, 2, 2 * HEAD_DIM, seq), BF16),
        jax.ShapeDtypeStruct((bsz, B_HEADS, seq, 2 * HEAD_DIM), BF16),
        jax.ShapeDtypeStruct((bsz, B_HEADS, ns, B_V_DIM, CHUNK), BF16),
    )
    out_specs = (
        pl.BlockSpec((1, A_Q_HEADS, 2 * HEAD_DIM, CHUNK), lambda b, s: (b, 0, 0, s)),
        pl.BlockSpec((1, CHUNK, 2 * HEAD_DIM), lambda b, s: (b, s, 0)),
        pl.BlockSpec((1, A_KV_HEADS, 1, HEAD_DIM, CHUNK),
                     lambda b, s: (b, 0, s, 0, 0)),
        pl.BlockSpec((1, B_HEADS, 2, 2 * HEAD_DIM, CHUNK), lambda b, s: (b, 0, 0, 0, s)),
        pl.BlockSpec((1, B_HEADS, CHUNK, 2 * HEAD_DIM), lambda b, s: (b, 0, s, 0)),
        pl.BlockSpec((1, B_HEADS, 1, B_V_DIM, CHUNK),
                     lambda b, s: (b, 0, s, 0, 0)),
    )
    return pl.pallas_call(
        _in_proj_kernel,
        grid=(bsz, ns),
        in_specs=[
            pl.BlockSpec((1, CHUNK, d), lambda b, s: (b, s, 0)),
            const((1, d)),
            const((IN_WIDTH, d)),
            const((HEAD_DIM, CHUNK)), const((HEAD_DIM, CHUNK)),
            const((HEAD_DIM, CHUNK)), const((HEAD_DIM, CHUNK)),
            tab, tab, tab, tab,
        ],
        out_specs=out_specs,
        out_shape=out_shape,
        scratch_shapes=[pltpu.VMEM((IN_WIDTH, CHUNK), F32)],
        compiler_params=pltpu.CompilerParams(
            dimension_semantics=("arbitrary", "arbitrary"),
            vmem_limit_bytes=VMEM_LIMIT),
        name="in_proj",
    )(x, g, w_t, gqa, gka, gqb, gkb, cr, sr, cc, sc)


def _softmax_step(s, m, l, v_t, acc_ref, idx):
    m_new = jnp.maximum(m, jnp.max(s, axis=0, keepdims=True))
    alpha = jnp.exp2(m - m_new)
    p = jnp.exp2(s - m_new)
    pv = jnp.dot(v_t, p.astype(BF16), preferred_element_type=F32)
    acc_ref[idx] = alpha * acc_ref[idx] + pv
    return m_new, alpha * l + jnp.sum(p, axis=0, keepdims=True)


def _sums_in_range(sums):
    good = jnp.logical_and(sums >= SUM_MIN, sums <= SUM_MAX)
    return jnp.min(jnp.where(good, 1.0, 0.0)) > 0.5


def _attention_core(q_of, k_chunk, v_chunk, bias_of, ref_bias, acc_ref, l_ref, p_ref, nk, zero):
    n_maps, _, tq = acc_ref.shape
    units = [(c, b0) for c in range(n_maps) for b0 in range(0, tq, QBLK)]
    cols = lambda b0: slice(b0, b0 + QBLK)

    def scores(j, c, b0):
        s = jnp.dot(k_chunk(j), q_of(c, cols(b0)), preferred_element_type=F32)
        return s if bias_of is None else s + bias_of(j, b0)

    def produce(j, slot, c, b0):
        p = jnp.exp2(scores(j, c, b0) - refs[c, b0])
        p_ref[slot, c, :, cols(b0)] = p.astype(BF16)
        l_ref[c, :, cols(b0)] += p.reshape(CHUNK // V7X_SUBLANES, V7X_SUBLANES, QBLK).sum(axis=0)

    def consume(j, slot, c, b0):
        acc_ref[c, :, cols(b0)] += jnp.dot(v_chunk(j), p_ref[slot, c, :, cols(b0)],
                                           preferred_element_type=F32)

    k_head = k_chunk(0)[0:REF_KEYS]
    refs = {(c, b0): jnp.max(jnp.dot(k_head, q_of(c, cols(b0)), preferred_element_type=F32),
                             axis=0, keepdims=True) + ref_bias for c, b0 in units}
    acc_ref[...] = jnp.zeros_like(acc_ref)
    l_ref[...] = jnp.zeros_like(l_ref)

    for c, b0 in units:
        produce(0, 0, c, b0)

    def body(ib, final):
        for t in range(2 * FAST_UNROLL):
            j = 2 * FAST_UNROLL * ib + t
            for c, b0 in units:
                if not (final and t == 2 * FAST_UNROLL - 1):
                    produce(j + 1, 1 - t % 2, c, b0)
                consume(j, t % 2, c, b0)

    n_bodies = nk // (2 * FAST_UNROLL)
    lax.fori_loop(0, zero + (n_bodies - 1), lambda ib, carry: body(ib, False) or carry, 0)
    body(n_bodies - 1, True)

    @pl.when(jnp.logical_not(_sums_in_range(jnp.sum(l_ref[...], axis=1))))
    def _():
        acc_ref[...] = jnp.zeros_like(acc_ref)

        def step(j, carry):
            v_t = v_chunk(j)
            return tuple(_softmax_step(scores(j, c, b0), m, l, v_t, acc_ref,
                                       (c, slice(None), cols(b0)))
                         for (c, b0), (m, l) in zip(units, carry))

        init = (jnp.full((1, QBLK), -jnp.inf, F32), jnp.zeros((1, QBLK), F32))
        final = lax.fori_loop(0, nk, step, (init,) * len(units))
        l_ref[...] = jnp.zeros_like(l_ref)
        for (c, b0), (_, l) in zip(units, final):
            l_ref[c, 0:1, cols(b0)] = l


def _kv_rows(j):
    return pl.ds(pl.multiple_of(j * CHUNK, CHUNK), CHUNK)


def _traced_zero():
    return jnp.minimum(pl.program_id(0), 0)


def _attn_a_kernel(q_ref, k_ref, v_ref, o_ref, acc_ref, l_ref, p_ref):
    _attention_core(
        q_of=lambda c, cols: q_ref[0, 0, :, cols],
        k_chunk=lambda j: k_ref[0, _kv_rows(j), :],
        v_chunk=lambda j: v_ref[0, 0, j],
        bias_of=None, ref_bias=0.0, acc_ref=acc_ref, l_ref=l_ref, p_ref=p_ref,
        nk=v_ref.shape[2], zero=_traced_zero())
    o_ref[0, 0] = (acc_ref[0] / jnp.sum(l_ref[0], axis=0, keepdims=True)).astype(BF16)


def _attn_a(qa, ka, va):
    bsz, _, _, seq = qa.shape
    ns = seq // CHUNK
    return pl.pallas_call(
        _attn_a_kernel,
        grid=(bsz, A_Q_HEADS, seq // TQ_A),
        in_specs=[
            pl.BlockSpec((1, 1, 2 * HEAD_DIM, TQ_A), lambda b, h, q: (b, h, 0, q)),
            pl.BlockSpec((1, seq, 2 * HEAD_DIM), lambda b, h, q: (b, 0, 0)),
            pl.BlockSpec((1, 1, ns, HEAD_DIM, CHUNK),
                         lambda b, h, q: (b, h // A_GROUP, 0, 0, 0)),
        ],
        out_specs=pl.BlockSpec((1, 1, HEAD_DIM, TQ_A), lambda b, h, q: (b, h, 0, q)),
        out_shape=jax.ShapeDtypeStruct((bsz, A_Q_HEADS, HEAD_DIM, seq), BF16),
        scratch_shapes=[pltpu.VMEM((1, HEAD_DIM, TQ_A), F32),
                        pltpu.VMEM((1, V7X_SUBLANES, TQ_A), F32),
                        pltpu.VMEM((2, 1, CHUNK, TQ_A), BF16)],
        compiler_params=pltpu.CompilerParams(
            dimension_semantics=("arbitrary", "arbitrary", "arbitrary"),
            vmem_limit_bytes=VMEM_LIMIT),
        name="attn_a",
    )(qa, ka, va)


def _attn_b_kernel(tbl_ref, q_ref, k_ref, v_ref, bias_ref, lq1_ref, lk1_ref, lq2_ref,
                   lk2_ref, g_ref, o_ref, acc_ref, l_ref, p_ref, *, lambda_init):
    h = pl.program_id(1)
    qi = pl.program_id(2)
    bias_max = tbl_ref[h, 0]
    for b in range(1, REL_BUCKETS):
        bias_max = jnp.maximum(bias_max, tbl_ref[h, b])
    far = BIAS_KINDS // 2
    first_block = qi * (TQ_B // CHUNK)
    _attention_core(
        q_of=lambda c, cols: q_ref[0, 0, c, :, cols],
        k_chunk=lambda j: k_ref[0, 0, _kv_rows(j), :],
        v_chunk=lambda j: v_ref[0, 0, j],
        bias_of=lambda j, b0: bias_ref[
            0, jnp.clip(j - (first_block + b0 // CHUNK), -far, far) + far],
        ref_bias=bias_max * LOG2E, acc_ref=acc_ref, l_ref=l_ref, p_ref=p_ref,
        nk=v_ref.shape[2], zero=_traced_zero())

    lam = (jnp.exp(jnp.sum(lq1_ref[...] * lk1_ref[...], keepdims=True))
           - jnp.exp(jnp.sum(lq2_ref[...] * lk2_ref[...], keepdims=True))
           + lambda_init)
    o = (acc_ref[0] / jnp.sum(l_ref[0], axis=0, keepdims=True)
         - lam * (acc_ref[1] / jnp.sum(l_ref[1], axis=0, keepdims=True)))
    msq = jnp.mean(o * o, axis=0, keepdims=True)
    o_ref[0, 0] = (o * lax.rsqrt(msq + EPS) * g_ref[...]).astype(BF16)


def _attn_b(tbl, qb, kb, vb, bias, lq1, lk1, lq2, lk2, gsub, lambda_init):
    bsz, _, _, _, seq = qb.shape
    ns = seq // CHUNK
    vec = pl.BlockSpec((1, HEAD_DIM), lambda b, h, q: (0, 0))
    return pl.pallas_call(
        functools.partial(_attn_b_kernel, lambda_init=lambda_init),
        grid=(bsz, B_HEADS, seq // TQ_B),
        in_specs=[
            pl.BlockSpec(memory_space=pltpu.SMEM),
            pl.BlockSpec((1, 1, 2, 2 * HEAD_DIM, TQ_B), lambda b, h, q: (b, h, 0, 0, q)),
            pl.BlockSpec((1, 1, seq, 2 * HEAD_DIM), lambda b, h, q: (b, h, 0, 0)),
            pl.BlockSpec((1, 1, ns, B_V_DIM, CHUNK),
                         lambda b, h, q: (b, h, 0, 0, 0)),
            pl.BlockSpec((1, BIAS_KINDS, CHUNK, CHUNK), lambda b, h, q: (h, 0, 0, 0)),
            vec, vec, vec, vec,
            pl.BlockSpec((B_V_DIM, TQ_B), lambda b, h, q: (0, 0)),
        ],
        out_specs=pl.BlockSpec((1, 1, B_V_DIM, TQ_B), lambda b, h, q: (b, h, 0, q)),
        out_shape=jax.ShapeDtypeStruct((bsz, B_HEADS, B_V_DIM, seq), BF16),
        scratch_shapes=[pltpu.VMEM((2, B_V_DIM, TQ_B), F32),
                        pltpu.VMEM((2, V7X_SUBLANES, TQ_B), F32),
                        pltpu.VMEM((2, 2, CHUNK, TQ_B), BF16)],
        compiler_params=pltpu.CompilerParams(
            dimension_semantics=("arbitrary", "arbitrary", "arbitrary"),
            vmem_limit_bytes=VMEM_LIMIT),
        name="attn_b",
    )(tbl, qb, kb, vb, bias, lq1, lk1, lq2, lk2, gsub)


def _out_mlp_kernel(x_ref, oa_ref, ob_ref, wo_ref, g_ref, wu_ref, wd_ref, y_ref):
    tn = (((0,), (0,)), ((), ()))
    attn = lax.dot_general(oa_ref[0], wo_ref[0:A_Q, :], tn, preferred_element_type=F32)
    attn = attn + lax.dot_general(ob_ref[0], wo_ref[A_Q:A_Q + B_V, :], tn,
                                  preferred_element_type=F32)
    x1 = x_ref[0] + attn
    ms = jnp.mean(x1 * x1, axis=-1, keepdims=True)
    h = (x1 * lax.rsqrt(ms + EPS) * g_ref[...]).astype(BF16)
    acc = x1
    for c in range(wu_ref.shape[1] // MLP_FCHUNK):
        cols = slice(c * MLP_FCHUNK, (c + 1) * MLP_FCHUNK)
        u = jnp.dot(h, wu_ref[:, cols], preferred_element_type=F32)
        u = jnp.square(jnp.maximum(u, 0.0)).astype(BF16)
        acc = acc + jnp.dot(u, wd_ref[cols, :], preferred_element_type=F32)
    y_ref[0] = acc


def _out_mlp(x, oa, ob, w_out, g2, w_up, w_down):
    bsz, seq, d = x.shape
    d_ff = w_up.shape[1]
    resident = lambda shape: pl.BlockSpec(shape, lambda b, s: (0,) * len(shape),
                                          pipeline_mode=pl.Buffered(1))
    return pl.pallas_call(
        _out_mlp_kernel,
        grid=(bsz, seq // MLP_ROWS),
        in_specs=[
            pl.BlockSpec((1, MLP_ROWS, d), lambda b, s: (b, s, 0)),
            pl.BlockSpec((1, A_Q, MLP_ROWS), lambda b, s: (b, 0, s)),
            pl.BlockSpec((1, B_V, MLP_ROWS), lambda b, s: (b, 0, s)),
            resident((A_Q + B_V, d)),
            resident((1, d)),
            resident((d, d_ff)),
            resident((d_ff, d)),
        ],
        out_specs=pl.BlockSpec((1, MLP_ROWS, d), lambda b, s: (b, s, 0)),
        out_shape=jax.ShapeDtypeStruct((bsz, seq, d), F32),
        compiler_params=pltpu.CompilerParams(
            dimension_semantics=("arbitrary", "arbitrary"),
            vmem_limit_bytes=VMEM_LIMIT),
        name="out_mlp",
    )(x, oa, ob, w_out, g2, w_up, w_down)


def _rope_tables_t(seq):
    pos = jnp.arange(seq)
    row = (pos // GRID_W).astype(F32)
    col = (pos % GRID_W).astype(F32)
    half = HEAD_DIM // 2
    inv_freq = 1.0 / (ROPE_THETA ** (jnp.arange(0, half, 2, dtype=F32) / half))
    ang_r = inv_freq[:, None] * row[None, :]
    ang_c = inv_freq[:, None] * col[None, :]
    return jnp.cos(ang_r), jnp.sin(ang_r), jnp.cos(ang_c), jnp.sin(ang_c)


def kernel(x, attn_norm_g, w_in, a_q_norm_g, a_k_norm_g, b_q_norm_g, b_k_norm_g,
           lambda_q1, lambda_k1, lambda_q2, lambda_k2, b_subln_g, rel_bias,
           w_out, mlp_norm_g, w_up, w_down):
    bsz, seq, d = x.shape
    depth = w_in.shape[0]
    assert seq % CHUNK == 0 and seq % TQ_A == 0 and seq % MLP_ROWS == 0
    assert w_in.shape[2] == IN_WIDTH and w_up.shape[2] % MLP_FCHUNK == 0
    cr, sr, cc, sc = _rope_tables_t(seq)
    qscale = HEAD_DIM ** -0.5 * LOG2E
    lanes = lambda v, n: jnp.broadcast_to(v.astype(F32)[:, None], (v.shape[0], n))
    bias = _bias_tiles(rel_bias.astype(F32).T)
    tbl = rel_bias.astype(F32).T
    for l in range(depth):
        lambda_init = 0.8 - 0.6 * math.exp(-0.3 * l)
        qa, ka, va, qb, kb, vb = _in_proj(
            x, attn_norm_g[l].astype(F32)[None, :], w_in[l].T.astype(BF16),
            lanes(a_q_norm_g[l] * qscale, CHUNK), lanes(a_k_norm_g[l], CHUNK),
            lanes(b_q_norm_g[l] * qscale, CHUNK), lanes(b_k_norm_g[l], CHUNK),
            cr, sr, cc, sc)
        oa = _attn_a(qa, ka, va)
        ob = _attn_b(tbl, qb, kb, vb, bias,
                     lambda_q1[l].astype(F32)[None, :], lambda_k1[l].astype(F32)[None, :],
                     lambda_q2[l].astype(F32)[None, :], lambda_k2[l].astype(F32)[None, :],
                     lanes(b_subln_g[l] * (1.0 - lambda_init), TQ_B), lambda_init)
        x = _out_mlp(x, oa.reshape(bsz, A_Q, seq), ob.reshape(bsz, B_V, seq),
                     w_out[l].astype(BF16), mlp_norm_g[l].astype(F32)[None, :],
                     w_up[l].astype(BF16), w_down[l].astype(BF16))
    return x
```

```python
import functools
import math

import jax
import jax.numpy as jnp
from jax import lax
from jax.experimental import pallas as pl
from jax.experimental.pallas import tpu as pltpu

HEAD_DIM = 64
A_Q_HEADS = 8
A_KV_HEADS = 2
A_GROUP = A_Q_HEADS // A_KV_HEADS
B_HEADS = 4
B_V_DIM = 2 * HEAD_DIM
A_Q = A_Q_HEADS * HEAD_DIM
A_KV = A_KV_HEADS * HEAD_DIM
B_QK = B_HEADS * 2 * HEAD_DIM
B_V = B_HEADS * B_V_DIM
GRID_W = 64
ROPE_THETA = 10000.0
REL_BUCKETS = 32
EPS = 1e-6
LOG2E = 1.4426950408889634
T5_BUCKET_STARTS = tuple(range(9)) + (12, 16, 23, 32, 46, 64, 91)

OFF_QA = 0
OFF_KA = OFF_QA + A_Q
OFF_VA = OFF_KA + A_KV
OFF_QB = OFF_VA + A_KV
OFF_KB = OFF_QB + B_QK
OFF_VB = OFF_KB + B_QK
IN_WIDTH = OFF_VB + B_V

BF16_SUBLANES = 16
ONES_ROWS = BF16_SUBLANES
CHUNK = 512
TQ_A = 2048
TQ_B = 1024
BIAS_KINDS = 5
MLP_ROWS = 512
MLP_FCHUNK = 1024
VMEM_LIMIT = 48 * 1024 * 1024
FAST_UNROLL = 2
QBLK = CHUNK
SUM_MIN = 2.0 ** -64
SUM_MAX = 2.0 ** 64

BF16 = jnp.bfloat16
F32 = jnp.float32


def _bias_tile_kernel(tbl_ref, out_ref):
    h = pl.program_id(0)
    d = pl.program_id(1) - BIAS_KINDS // 2
    kk = lax.broadcasted_iota(jnp.int32, (CHUNK, CHUNK), 0)
    qq = lax.broadcasted_iota(jnp.int32, (CHUNK, CHUNK), 1)
    rel = d * CHUNK + kk - qq
    n = jnp.abs(rel)

    def side(first_bucket):
        v = jnp.full((CHUNK, CHUNK), tbl_ref[h, first_bucket] * LOG2E, F32)
        for b, start in enumerate(T5_BUCKET_STARTS[1:], 1):
            v = jnp.where(n >= start, tbl_ref[h, first_bucket + b] * LOG2E, v)
        return v

    @pl.when(d < 0)
    def _():
        out_ref[0, 0] = side(0)

    @pl.when(d > 0)
    def _():
        out_ref[0, 0] = side(REL_BUCKETS // 2)

    @pl.when(d == 0)
    def _():
        out_ref[0, 0] = jnp.where(rel > 0, side(REL_BUCKETS // 2), side(0))


def _bias_tiles(rel_bias_t):
    return pl.pallas_call(
        _bias_tile_kernel,
        grid=(B_HEADS, BIAS_KINDS),
        in_specs=[pl.BlockSpec(memory_space=pltpu.SMEM)],
        out_specs=pl.BlockSpec((1, 1, CHUNK, CHUNK), lambda h, d: (h, d, 0, 0)),
        out_shape=jax.ShapeDtypeStruct((B_HEADS, BIAS_KINDS, CHUNK, CHUNK), F32),
        name="bias_tiles",
    )(rel_bias_t)


def _in_proj_kernel(x_ref, g_ref, w_ref, gqa_ref, gka_ref, gqb_ref, gkb_ref,
                    cr_ref, sr_ref, cc_ref, sc_ref,
                    qa_ref, ka_ref, va_ref, qb_ref, kb_ref, vb_ref, pt_ref):
    x = x_ref[0]
    ms = jnp.mean(x * x, axis=-1, keepdims=True)
    h = (x * lax.rsqrt(ms + EPS) * g_ref[...]).astype(BF16)
    cr, sr, cc, sc = cr_ref[...], sr_ref[...], cc_ref[...], sc_ref[...]
    quarter = HEAD_DIM // 4

    def project(r0, n):
        pt_ref[r0:r0 + n, :] = lax.dot_general(
            w_ref[r0:r0 + n, :], h, (((1,), (1,)), ((), ())), preferred_element_type=F32)

    def norm(r0, gain_ref):
        t = pt_ref[r0:r0 + HEAD_DIM, :]
        m = jnp.mean(t * t, axis=0, keepdims=True)
        return t * lax.rsqrt(m + EPS) * gain_ref[...]

    def rope(y):
        x1r, x2r = y[0:quarter], y[quarter:2 * quarter]
        x1c, x2c = y[2 * quarter:3 * quarter], y[3 * quarter:]
        return jnp.concatenate(
            [x1r * cr - x2r * sr, x2r * cr + x1r * sr,
             x1c * cc - x2c * sc, x2c * cc + x1c * sc], axis=0)

    zeros = jnp.zeros((HEAD_DIM, CHUNK), BF16)
    ones = jnp.ones((ONES_ROWS, CHUNK), BF16)

    def finish_qa():
        for hh in range(A_Q_HEADS):
            q = rope(norm(OFF_QA + hh * HEAD_DIM, gqa_ref)).astype(BF16)
            slot = hh // A_GROUP
            qa_ref[0, hh, slot * HEAD_DIM:(slot + 1) * HEAD_DIM, :] = q
            qa_ref[0, hh, (1 - slot) * HEAD_DIM:(2 - slot) * HEAD_DIM, :] = zeros

    def finish_kva():
        kt = jnp.concatenate([rope(norm(OFF_KA + kv * HEAD_DIM, gka_ref))
                              for kv in range(A_KV_HEADS)], axis=0)
        ka_ref[0] = kt.T.astype(BF16)
        for kv in range(A_KV_HEADS):
            r0 = OFF_VA + kv * HEAD_DIM
            va_ref[0, kv, 0, 0:HEAD_DIM, :] = pt_ref[r0:r0 + HEAD_DIM, :].astype(BF16)
            va_ref[0, kv, 0, HEAD_DIM:HEAD_DIM + ONES_ROWS, :] = ones

    def finish_qb():
        for hb in range(B_HEADS):
            for c in range(2):
                q = norm(OFF_QB + (hb * 2 + c) * HEAD_DIM, gqb_ref).astype(BF16)
                qb_ref[0, hb, c, c * HEAD_DIM:(c + 1) * HEAD_DIM, :] = q
                qb_ref[0, hb, c, (1 - c) * HEAD_DIM:(2 - c) * HEAD_DIM, :] = zeros

    def finish_kb():
        for hb in range(B_HEADS):
            kt = jnp.concatenate([norm(OFF_KB + (hb * 2 + c) * HEAD_DIM, gkb_ref)
                                  for c in range(2)], axis=0)
            kb_ref[0, hb] = kt.T.astype(BF16)

    def finish_vb():
        for hb in range(B_HEADS):
            r0 = OFF_VB + hb * B_V_DIM
            vb_ref[0, hb, 0, 0:B_V_DIM, :] = pt_ref[r0:r0 + B_V_DIM, :].astype(BF16)
            vb_ref[0, hb, 0, B_V_DIM:B_V_DIM + ONES_ROWS, :] = ones

    project(OFF_QA, A_Q)
    project(OFF_KA, 2 * A_KV)
    finish_qa()
    project(OFF_QB, B_QK)
    finish_kva()
    project(OFF_KB, B_QK)
    finish_qb()
    project(OFF_VB, B_V)
    finish_kb()
    finish_vb()


def _in_proj(x, g, w_t, gqa, gka, gqb, gkb, cr, sr, cc, sc):
    bsz, seq, d = x.shape
    ns = seq // CHUNK
    const = lambda shape: pl.BlockSpec(shape, lambda b, s: (0,) * len(shape))
    tab = pl.BlockSpec((HEAD_DIM // 4, CHUNK), lambda b, s: (0, s))
    out_shape = (
        jax.ShapeDtypeStruct((bsz, A_Q_HEADS, 2 * HEAD_DIM, seq), BF16),
        jax.ShapeDtypeStruct((bsz, seq, 2 * HEAD_DIM), BF16),
        jax.ShapeDtypeStruct((bsz, A_KV_HEADS, ns, HEAD_DIM + ONES_ROWS, CHUNK), BF16),
        jax.ShapeDtypeStruct((bsz, B_HEADS, 2, 2 * HEAD_DIM, seq), BF16),
        jax.ShapeDtypeStruct((bsz, B_HEADS, seq, 2 * HEAD_DIM), BF16),
        jax.ShapeDtypeStruct((bsz, B_HEADS, ns, B_V_DIM + ONES_ROWS, CHUNK), BF16),
    )
    out_specs = (
        pl.BlockSpec((1, A_Q_HEADS, 2 * HEAD_DIM, CHUNK), lambda b, s: (b, 0, 0, s)),
        pl.BlockSpec((1, CHUNK, 2 * HEAD_DIM), lambda b, s: (b, s, 0)),
        pl.BlockSpec((1, A_KV_HEADS, 1, HEAD_DIM + ONES_ROWS, CHUNK),
                     lambda b, s: (b, 0, s, 0, 0)),
        pl.BlockSpec((1, B_HEADS, 2, 2 * HEAD_DIM, CHUNK), lambda b, s: (b, 0, 0, 0, s)),
        pl.BlockSpec((1, B_HEADS, CHUNK, 2 * HEAD_DIM), lambda b, s: (b, 0, s, 0)),
        pl.BlockSpec((1, B_HEADS, 1, B_V_DIM + ONES_ROWS, CHUNK),
                     lambda b, s: (b, 0, s, 0, 0)),
    )
    return pl.pallas_call(
        _in_proj_kernel,
        grid=(bsz, ns),
        in_specs=[
            pl.BlockSpec((1, CHUNK, d), lambda b, s: (b, s, 0)),
            const((1, d)),
            const((IN_WIDTH, d)),
            const((HEAD_DIM, CHUNK)), const((HEAD_DIM, CHUNK)),
            const((HEAD_DIM, CHUNK)), const((HEAD_DIM, CHUNK)),
            tab, tab, tab, tab,
        ],
        out_specs=out_specs,
        out_shape=out_shape,
        scratch_shapes=[pltpu.VMEM((IN_WIDTH, CHUNK), F32)],
        compiler_params=pltpu.CompilerParams(
            dimension_semantics=("arbitrary", "arbitrary"),
            vmem_limit_bytes=VMEM_LIMIT),
        name="in_proj",
    )(x, g, w_t, gqa, gka, gqb, gkb, cr, sr, cc, sc)


def _softmax_step(s, m, v_t, acc_ref, idx):
    m_new = jnp.maximum(m, jnp.max(s, axis=0, keepdims=True))
    alpha = jnp.exp2(m - m_new)
    p = jnp.exp2(s - m_new).astype(BF16)
    pv = jnp.dot(v_t, p, preferred_element_type=F32)
    acc_ref[idx] = alpha * acc_ref[idx] + pv
    return m_new


def _sums_in_range(sums):
    good = jnp.logical_and(sums >= SUM_MIN, sums <= SUM_MAX)
    return jnp.min(jnp.where(good, 1.0, 0.0)) > 0.5


def _attention_core(q_of, k_chunk, v_chunk, bias_of, acc_ref, p_ref, nk, zero):
    n_maps, rows, tq = acc_ref.shape
    sum_row = rows - ONES_ROWS
    units = [(c, b0) for c in range(n_maps) for b0 in range(0, tq, QBLK)]
    cols = lambda b0: slice(b0, b0 + QBLK)

    def scores(j, c, b0):
        s = jnp.dot(k_chunk(j), q_of(c, cols(b0)), preferred_element_type=F32)
        return s if bias_of is None else s + bias_of(j, b0)

    def produce(j, slot, c, b0):
        p_ref[slot, c, :, cols(b0)] = jnp.exp2(scores(j, c, b0)).astype(BF16)

    def consume(j, slot, c, b0):
        acc_ref[c, :, cols(b0)] += jnp.dot(v_chunk(j), p_ref[slot, c, :, cols(b0)],
                                           preferred_element_type=F32)

    acc_ref[...] = jnp.zeros_like(acc_ref)

    for c, b0 in units:
        produce(0, 0, c, b0)

    def body(ib, final):
        for t in range(2 * FAST_UNROLL):
            j = 2 * FAST_UNROLL * ib + t
            for c, b0 in units:
                if not (final and t == 2 * FAST_UNROLL - 1):
                    produce(j + 1, 1 - t % 2, c, b0)
                consume(j, t % 2, c, b0)

    n_bodies = nk // (2 * FAST_UNROLL)
    lax.fori_loop(0, zero + (n_bodies - 1), lambda ib, carry: body(ib, False) or carry, 0)
    body(n_bodies - 1, True)

    @pl.when(jnp.logical_not(_sums_in_range(acc_ref[:, sum_row:sum_row + 1, :])))
    def _():
        acc_ref[...] = jnp.zeros_like(acc_ref)

        def step(j, ms):
            v_t = v_chunk(j)
            return tuple(_softmax_step(scores(j, c, b0), m, v_t, acc_ref,
                                       (c, slice(None), cols(b0)))
                         for (c, b0), m in zip(units, ms))

        lax.fori_loop(0, nk, step, (jnp.full((1, QBLK), -jnp.inf, F32),) * len(units))


def _kv_rows(j):
    return pl.ds(pl.multiple_of(j * CHUNK, CHUNK), CHUNK)


def _traced_zero():
    return jnp.minimum(pl.program_id(0), 0)


def _attn_a_kernel(q_ref, k_ref, v_ref, o_ref, acc_ref, p_ref):
    _attention_core(
        q_of=lambda c, cols: q_ref[0, 0, :, cols],
        k_chunk=lambda j: k_ref[0, _kv_rows(j), :],
        v_chunk=lambda j: v_ref[0, 0, j],
        bias_of=None, acc_ref=acc_ref, p_ref=p_ref, nk=v_ref.shape[2],
        zero=_traced_zero())
    acc = acc_ref[0]
    o_ref[0, 0] = (acc[0:HEAD_DIM] / acc[HEAD_DIM:HEAD_DIM + 1]).astype(BF16)


def _attn_a(qa, ka, va):
    bsz, _, _, seq = qa.shape
    ns = seq // CHUNK
    return pl.pallas_call(
        _attn_a_kernel,
        grid=(bsz, A_Q_HEADS, seq // TQ_A),
        in_specs=[
            pl.BlockSpec((1, 1, 2 * HEAD_DIM, TQ_A), lambda b, h, q: (b, h, 0, q)),
            pl.BlockSpec((1, seq, 2 * HEAD_DIM), lambda b, h, q: (b, 0, 0)),
            pl.BlockSpec((1, 1, ns, HEAD_DIM + ONES_ROWS, CHUNK),
                         lambda b, h, q: (b, h // A_GROUP, 0, 0, 0)),
        ],
        out_specs=pl.BlockSpec((1, 1, HEAD_DIM, TQ_A), lambda b, h, q: (b, h, 0, q)),
        out_shape=jax.ShapeDtypeStruct((bsz, A_Q_HEADS, HEAD_DIM, seq), BF16),
        scratch_shapes=[pltpu.VMEM((1, HEAD_DIM + ONES_ROWS, TQ_A), F32),
                        pltpu.VMEM((2, 1, CHUNK, TQ_A), BF16)],
        compiler_params=pltpu.CompilerParams(
            dimension_semantics=("arbitrary", "arbitrary", "arbitrary"),
            vmem_limit_bytes=VMEM_LIMIT),
        name="attn_a",
    )(qa, ka, va)


def _attn_b_kernel(q_ref, k_ref, v_ref, bias_ref, lq1_ref, lk1_ref, lq2_ref,
                   lk2_ref, g_ref, o_ref, acc_ref, p_ref, *, lambda_init):
    qi = pl.program_id(2)
    far = BIAS_KINDS // 2
    first_block = qi * (TQ_B // CHUNK)
    _attention_core(
        q_of=lambda c, cols: q_ref[0, 0, c, :, cols],
        k_chunk=lambda j: k_ref[0, 0, _kv_rows(j), :],
        v_chunk=lambda j: v_ref[0, 0, j],
        bias_of=lambda j, b0: bias_ref[
            0, jnp.clip(j - (first_block + b0 // CHUNK), -far, far) + far],
        acc_ref=acc_ref, p_ref=p_ref, nk=v_ref.shape[2], zero=_traced_zero())

    lam = (jnp.exp(jnp.sum(lq1_ref[...] * lk1_ref[...], keepdims=True))
           - jnp.exp(jnp.sum(lq2_ref[...] * lk2_ref[...], keepdims=True))
           + lambda_init)
    a0, a1 = acc_ref[0], acc_ref[1]
    o = (a0[0:B_V_DIM] / a0[B_V_DIM:B_V_DIM + 1]
         - lam * (a1[0:B_V_DIM] / a1[B_V_DIM:B_V_DIM + 1]))
    msq = jnp.mean(o * o, axis=0, keepdims=True)
    o_ref[0, 0] = (o * lax.rsqrt(msq + EPS) * g_ref[...]).astype(BF16)


def _attn_b(qb, kb, vb, bias, lq1, lk1, lq2, lk2, gsub, lambda_init):
    bsz, _, _, _, seq = qb.shape
    ns = seq // CHUNK
    vec = pl.BlockSpec((1, HEAD_DIM), lambda b, h, q: (0, 0))
    return pl.pallas_call(
        functools.partial(_attn_b_kernel, lambda_init=lambda_init),
        grid=(bsz, B_HEADS, seq // TQ_B),
        in_specs=[
            pl.BlockSpec((1, 1, 2, 2 * HEAD_DIM, TQ_B), lambda b, h, q: (b, h, 0, 0, q)),
            pl.BlockSpec((1, 1, seq, 2 * HEAD_DIM), lambda b, h, q: (b, h, 0, 0)),
            pl.BlockSpec((1, 1, ns, B_V_DIM + ONES_ROWS, CHUNK),
                         lambda b, h, q: (b, h, 0, 0, 0)),
            pl.BlockSpec((1, BIAS_KINDS, CHUNK, CHUNK), lambda b, h, q: (h, 0, 0, 0)),
            vec, vec, vec, vec,
            pl.BlockSpec((B_V_DIM, TQ_B), lambda b, h, q: (0, 0)),
        ],
        out_specs=pl.BlockSpec((1, 1, B_V_DIM, TQ_B), lambda b, h, q: (b, h, 0, q)),
        out_shape=jax.ShapeDtypeStruct((bsz, B_HEADS, B_V_DIM, seq), BF16),
        scratch_shapes=[pltpu.VMEM((2, B_V_DIM + ONES_ROWS, TQ_B), F32),
                        pltpu.VMEM((2, 2, CHUNK, TQ_B), BF16)],
        compiler_params=pltpu.CompilerParams(
            dimension_semantics=("arbitrary", "arbitrary", "arbitrary"),
            vmem_limit_bytes=VMEM_LIMIT),
        name="attn_b",
    )(qb, kb, vb, bias, lq1, lk1, lq2, lk2, gsub)


def _out_mlp_kernel(x_ref, oa_ref, ob_ref, wo_ref, g_ref, wu_ref, wd_ref, y_ref):
    tn = (((0,), (0,)), ((), ()))
    attn = lax.dot_general(oa_ref[0], wo_ref[0:A_Q, :], tn, preferred_element_type=F32)
    attn = attn + lax.dot_general(ob_ref[0], wo_ref[A_Q:A_Q + B_V, :], tn,
                                  preferred_element_type=F32)
    x1 = x_ref[0] + attn
    ms = jnp.mean(x1 * x1, axis=-1, keepdims=True)
    h = (x1 * lax.rsqrt(ms + EPS) * g_ref[...]).astype(BF16)
    acc = x1
    for c in range(wu_ref.shape[1] // MLP_FCHUNK):
        cols = slice(c * MLP_FCHUNK, (c + 1) * MLP_FCHUNK)
        u = jnp.dot(h, wu_ref[:, cols], preferred_element_type=F32)
        u = jnp.square(jnp.maximum(u, 0.0)).astype(BF16)
        acc = acc + jnp.dot(u, wd_ref[cols, :], preferred_element_type=F32)
    y_ref[0] = acc


def _out_mlp(x, oa, ob, w_out, g2, w_up, w_down):
    bsz, seq, d = x.shape
    d_ff = w_up.shape[1]
    resident = lambda shape: pl.BlockSpec(shape, lambda b, s: (0,) * len(shape),
                                          pipeline_mode=pl.Buffered(1))
    return pl.pallas_call(
        _out_mlp_kernel,
        grid=(bsz, seq // MLP_ROWS),
        in_specs=[
            pl.BlockSpec((1, MLP_ROWS, d), lambda b, s: (b, s, 0)),
            pl.BlockSpec((1, A_Q, MLP_ROWS), lambda b, s: (b, 0, s)),
            pl.BlockSpec((1, B_V, MLP_ROWS), lambda b, s: (b, 0, s)),
            resident((A_Q + B_V, d)),
            resident((1, d)),
            resident((d, d_ff)),
            resident((d_ff, d)),
        ],
        out_specs=pl.BlockSpec((1, MLP_ROWS, d), lambda b, s: (b, s, 0)),
        out_shape=jax.ShapeDtypeStruct((bsz, seq, d), F32),
        compiler_params=pltpu.CompilerParams(
            dimension_semantics=("arbitrary", "arbitrary"),
            vmem_limit_bytes=VMEM_LIMIT),
        name="out_mlp",
    )(x, oa, ob, w_out, g2, w_up, w_down)


def _rope_tables_t(seq):
    pos = jnp.arange(seq)
    row = (pos // GRID_W).astype(F32)
    col = (pos % GRID_W).astype(F32)
    half = HEAD_DIM // 2
    inv_freq = 1.0 / (ROPE_THETA ** (jnp.arange(0, half, 2, dtype=F32) / half))
    ang_r = inv_freq[:, None] * row[None, :]
    ang_c = inv_freq[:, None] * col[None, :]
    return jnp.cos(ang_r), jnp.sin(ang_r), jnp.cos(ang_c), jnp.sin(ang_c)


def kernel(x, attn_norm_g, w_in, a_q_norm_g, a_k_norm_g, b_q_norm_g, b_k_norm_g,
           lambda_q1, lambda_k1, lambda_q2, lambda_k2, b_subln_g, rel_bias,
           w_out, mlp_norm_g, w_up, w_down):
    bsz, seq, d = x.shape
    depth = w_in.shape[0]
    assert seq % CHUNK == 0 and seq % TQ_A == 0 and seq % MLP_ROWS == 0
    assert w_in.shape[2] == IN_WIDTH and w_up.shape[2] % MLP_FCHUNK == 0
    cr, sr, cc, sc = _rope_tables_t(seq)
    qscale = HEAD_DIM ** -0.5 * LOG2E
    lanes = lambda v, n: jnp.broadcast_to(v.astype(F32)[:, None], (v.shape[0], n))
    bias = _bias_tiles(rel_bias.astype(F32).T)
    for l in range(depth):
        lambda_init = 0.8 - 0.6 * math.exp(-0.3 * l)
        qa, ka, va, qb, kb, vb = _in_proj(
            x, attn_norm_g[l].astype(F32)[None, :], w_in[l].T.astype(BF16),
            lanes(a_q_norm_g[l] * qscale, CHUNK), lanes(a_k_norm_g[l], CHUNK),
            lanes(b_q_norm_g[l] * qscale, CHUNK), lanes(b_k_norm_g[l], CHUNK),
            cr, sr, cc, sc)
        oa = _attn_a(qa, ka, va)
        ob = _attn_b(qb, kb, vb, bias,
                     lambda_q1[l].astype(F32)[None, :], lambda_k1[l].astype(F32)[None, :],
                     lambda_q2[l].astype(F32)[None, :], lambda_k2[l].astype(F32)[None, :],
                     lanes(b_subln_g[l] * (1.0 - lambda_init), TQ_B), lambda_init)
        x = _out_mlp(x, oa.reshape(bsz, A_Q, seq), ob.reshape(bsz, B_V, seq),
                     w_out[l].astype(BF16), mlp_norm_g[l].astype(F32)[None, :],
                     w_up[l].astype(BF16), w_down[l].astype(BF16))
    return x
```

```python
import functools
import math

import jax
import jax.numpy as jnp
from jax import lax
from jax.experimental import pallas as pl
from jax.experimental.pallas import tpu as pltpu

HEAD_DIM = 64
A_Q_HEADS = 8
A_KV_HEADS = 2
A_GROUP = A_Q_HEADS // A_KV_HEADS
B_HEADS = 4
B_V_DIM = 2 * HEAD_DIM
A_Q = A_Q_HEADS * HEAD_DIM
A_KV = A_KV_HEADS * HEAD_DIM
B_QK = B_HEADS * 2 * HEAD_DIM
B_V = B_HEADS * B_V_DIM
GRID_W = 64
ROPE_THETA = 10000.0
REL_BUCKETS = 32
EPS = 1e-6
LOG2E = 1.4426950408889634
T5_BUCKET_STARTS = tuple(range(9)) + (12, 16, 23, 32, 46, 64, 91)

OFF_QA = 0
OFF_KA = OFF_QA + A_Q
OFF_VA = OFF_KA + A_KV
OFF_QB = OFF_VA + A_KV
OFF_KB = OFF_QB + B_QK
OFF_VB = OFF_KB + B_QK
IN_WIDTH = OFF_VB + B_V

BF16_SUBLANES = 16
ONES_ROWS = BF16_SUBLANES
CHUNK = 512
TQ_A = 2048
TQ_B = 1024
BIAS_KINDS = 5
MLP_ROWS = 512
MLP_FCHUNK = 1024
VMEM_LIMIT = 48 * 1024 * 1024
FAST_UNROLL = 2
KSTRIP = 256
QBLK = CHUNK
SUM_MIN = 2.0 ** -64
SUM_MAX = 2.0 ** 64

BF16 = jnp.bfloat16
F32 = jnp.float32


def _bias_tile_kernel(tbl_ref, out_ref):
    h = pl.program_id(0)
    d = pl.program_id(1) - BIAS_KINDS // 2
    kk = lax.broadcasted_iota(jnp.int32, (CHUNK, CHUNK), 0)
    qq = lax.broadcasted_iota(jnp.int32, (CHUNK, CHUNK), 1)
    rel = d * CHUNK + kk - qq
    n = jnp.abs(rel)

    def side(first_bucket):
        v = jnp.full((CHUNK, CHUNK), tbl_ref[h, first_bucket] * LOG2E, F32)
        for b, start in enumerate(T5_BUCKET_STARTS[1:], 1):
            v = jnp.where(n >= start, tbl_ref[h, first_bucket + b] * LOG2E, v)
        return v

    @pl.when(d < 0)
    def _():
        out_ref[0, 0] = side(0)

    @pl.when(d > 0)
    def _():
        out_ref[0, 0] = side(REL_BUCKETS // 2)

    @pl.when(d == 0)
    def _():
        out_ref[0, 0] = jnp.where(rel > 0, side(REL_BUCKETS // 2), side(0))


def _bias_tiles(rel_bias_t):
    return pl.pallas_call(
        _bias_tile_kernel,
        grid=(B_HEADS, BIAS_KINDS),
        in_specs=[pl.BlockSpec(memory_space=pltpu.SMEM)],
        out_specs=pl.BlockSpec((1, 1, CHUNK, CHUNK), lambda h, d: (h, d, 0, 0)),
        out_shape=jax.ShapeDtypeStruct((B_HEADS, BIAS_KINDS, CHUNK, CHUNK), F32),
        name="bias_tiles",
    )(rel_bias_t)


def _in_proj_kernel(x_ref, g_ref, w_ref, gqa_ref, gka_ref, gqb_ref, gkb_ref,
                    cr_ref, sr_ref, cc_ref, sc_ref,
                    qa_ref, ka_ref, va_ref, qb_ref, kb_ref, vb_ref, pt_ref):
    x = x_ref[0]
    ms = jnp.mean(x * x, axis=-1, keepdims=True)
    h = (x * lax.rsqrt(ms + EPS) * g_ref[...]).astype(BF16)
    cr, sr, cc, sc = cr_ref[...], sr_ref[...], cc_ref[...], sc_ref[...]
    quarter = HEAD_DIM // 4

    def project(r0, n):
        pt_ref[r0:r0 + n, :] = lax.dot_general(
            w_ref[r0:r0 + n, :], h, (((1,), (1,)), ((), ())), preferred_element_type=F32)

    def norm(r0, gain_ref):
        t = pt_ref[r0:r0 + HEAD_DIM, :]
        m = jnp.mean(t * t, axis=0, keepdims=True)
        return t * lax.rsqrt(m + EPS) * gain_ref[...]

    def rope(y):
        x1r, x2r = y[0:quarter], y[quarter:2 * quarter]
        x1c, x2c = y[2 * quarter:3 * quarter], y[3 * quarter:]
        return jnp.concatenate(
            [x1r * cr - x2r * sr, x2r * cr + x1r * sr,
             x1c * cc - x2c * sc, x2c * cc + x1c * sc], axis=0)

    zeros = jnp.zeros((HEAD_DIM, CHUNK), BF16)
    ones = jnp.ones((ONES_ROWS, CHUNK), BF16)

    def finish_qa():
        for hh in range(A_Q_HEADS):
            q = rope(norm(OFF_QA + hh * HEAD_DIM, gqa_ref)).astype(BF16)
            slot = hh // A_GROUP
            qa_ref[0, hh, slot * HEAD_DIM:(slot + 1) * HEAD_DIM, :] = q
            qa_ref[0, hh, (1 - slot) * HEAD_DIM:(2 - slot) * HEAD_DIM, :] = zeros

    def finish_kva():
        kt = jnp.concatenate([rope(norm(OFF_KA + kv * HEAD_DIM, gka_ref))
                              for kv in range(A_KV_HEADS)], axis=0)
        ka_ref[0] = kt.T.astype(BF16)
        for kv in range(A_KV_HEADS):
            r0 = OFF_VA + kv * HEAD_DIM
            va_ref[0, kv, 0, 0:HEAD_DIM, :] = pt_ref[r0:r0 + HEAD_DIM, :].astype(BF16)
            va_ref[0, kv, 0, HEAD_DIM:HEAD_DIM + ONES_ROWS, :] = ones

    def finish_qb():
        for hb in range(B_HEADS):
            for c in range(2):
                q = norm(OFF_QB + (hb * 2 + c) * HEAD_DIM, gqb_ref).astype(BF16)
                qb_ref[0, hb, c, c * HEAD_DIM:(c + 1) * HEAD_DIM, :] = q
                qb_ref[0, hb, c, (1 - c) * HEAD_DIM:(2 - c) * HEAD_DIM, :] = zeros

    def finish_kb():
        for hb in range(B_HEADS):
            kt = jnp.concatenate([norm(OFF_KB + (hb * 2 + c) * HEAD_DIM, gkb_ref)
                                  for c in range(2)], axis=0)
            kb_ref[0, hb] = kt.T.astype(BF16)

    def finish_vb():
        for hb in range(B_HEADS):
            r0 = OFF_VB + hb * B_V_DIM
            vb_ref[0, hb, 0, 0:B_V_DIM, :] = pt_ref[r0:r0 + B_V_DIM, :].astype(BF16)
            vb_ref[0, hb, 0, B_V_DIM:B_V_DIM + ONES_ROWS, :] = ones

    project(OFF_QA, A_Q)
    project(OFF_KA, 2 * A_KV)
    finish_qa()
    project(OFF_QB, B_QK)
    finish_kva()
    project(OFF_KB, B_QK)
    finish_qb()
    project(OFF_VB, B_V)
    finish_kb()
    finish_vb()


def _in_proj(x, g, w_t, gqa, gka, gqb, gkb, cr, sr, cc, sc):
    bsz, seq, d = x.shape
    ns = seq // CHUNK
    const = lambda shape: pl.BlockSpec(shape, lambda b, s: (0,) * len(shape))
    tab = pl.BlockSpec((HEAD_DIM // 4, CHUNK), lambda b, s: (0, s))
    out_shape = (
        jax.ShapeDtypeStruct((bsz, A_Q_HEADS, 2 * HEAD_DIM, seq), BF16),
        jax.ShapeDtypeStruct((bsz, seq, 2 * HEAD_DIM), BF16),
        jax.ShapeDtypeStruct((bsz, A_KV_HEADS, ns, HEAD_DIM + ONES_ROWS, CHUNK), BF16),
        jax.ShapeDtypeStruct((bsz, B_HEADS, 2, 2 * HEAD_DIM, seq), BF16),
        jax.ShapeDtypeStruct((bsz, B_HEADS, seq, 2 * HEAD_DIM), BF16),
        jax.ShapeDtypeStruct((bsz, B_HEADS, ns, B_V_DIM + ONES_ROWS, CHUNK), BF16),
    )
    out_specs = (
        pl.BlockSpec((1, A_Q_HEADS, 2 * HEAD_DIM, CHUNK), lambda b, s: (b, 0, 0, s)),
        pl.BlockSpec((1, CHUNK, 2 * HEAD_DIM), lambda b, s: (b, s, 0)),
        pl.BlockSpec((1, A_KV_HEADS, 1, HEAD_DIM + ONES_ROWS, CHUNK),
                     lambda b, s: (b, 0, s, 0, 0)),
        pl.BlockSpec((1, B_HEADS, 2, 2 * HEAD_DIM, CHUNK), lambda b, s: (b, 0, 0, 0, s)),
        pl.BlockSpec((1, B_HEADS, CHUNK, 2 * HEAD_DIM), lambda b, s: (b, 0, s, 0)),
        pl.BlockSpec((1, B_HEADS, 1, B_V_DIM + ONES_ROWS, CHUNK),
                     lambda b, s: (b, 0, s, 0, 0)),
    )
    return pl.pallas_call(
        _in_proj_kernel,
        grid=(bsz, ns),
        in_specs=[
            pl.BlockSpec((1, CHUNK, d), lambda b, s: (b, s, 0)),
            const((1, d)),
            const((IN_WIDTH, d)),
            const((HEAD_DIM, CHUNK)), const((HEAD_DIM, CHUNK)),
            const((HEAD_DIM, CHUNK)), const((HEAD_DIM, CHUNK)),
            tab, tab, tab, tab,
        ],
        out_specs=out_specs,
        out_shape=out_shape,
        scratch_shapes=[pltpu.VMEM((IN_WIDTH, CHUNK), F32)],
        compiler_params=pltpu.CompilerParams(
            dimension_semantics=("arbitrary", "arbitrary"),
            vmem_limit_bytes=VMEM_LIMIT),
        name="in_proj",
    )(x, g, w_t, gqa, gka, gqb, gkb, cr, sr, cc, sc)


def _softmax_step(s, m, v_t, acc_ref, idx):
    m_new = jnp.maximum(m, jnp.max(s, axis=0, keepdims=True))
    alpha = jnp.exp2(m - m_new)
    p = jnp.exp2(s - m_new).astype(BF16)
    pv = jnp.dot(v_t, p, preferred_element_type=F32)
    acc_ref[idx] = alpha * acc_ref[idx] + pv
    return m_new


def _sums_in_range(sums):
    good = jnp.logical_and(sums >= SUM_MIN, sums <= SUM_MAX)
    return jnp.min(jnp.where(good, 1.0, 0.0)) > 0.5


def _attention_core(q_of, k_strip, v_strip, bias_of, acc_ref, p_ref, nk, zero):
    n_maps, rows, tq = acc_ref.shape
    sum_row = rows - ONES_ROWS
    units = [(c, b0, st) for c in range(n_maps) for b0 in range(0, tq, QBLK)
             for st in range(CHUNK // KSTRIP)]
    cols = lambda b0: slice(b0, b0 + QBLK)
    keys = lambda st: slice(st * KSTRIP, (st + 1) * KSTRIP)

    def scores(j, c, b0, st):
        s = jnp.dot(k_strip(j, st), q_of(c, cols(b0), st), preferred_element_type=F32)
        return s if bias_of is None else s + bias_of(j, b0, st)

    def produce(j, slot, c, b0, st):
        p_ref[slot, c, keys(st), cols(b0)] = jnp.exp2(scores(j, c, b0, st)).astype(BF16)

    def consume(j, slot, c, b0, st):
        acc_ref[c, :, cols(b0)] += jnp.dot(v_strip(j, st), p_ref[slot, c, keys(st), cols(b0)],
                                           preferred_element_type=F32)

    acc_ref[...] = jnp.zeros_like(acc_ref)

    for unit in units:
        produce(0, 0, *unit)

    def body(ib, final):
        for t in range(2 * FAST_UNROLL):
            j = 2 * FAST_UNROLL * ib + t
            for unit in units:
                if not (final and t == 2 * FAST_UNROLL - 1):
                    produce(j + 1, 1 - t % 2, *unit)
                consume(j, t % 2, *unit)

    n_bodies = nk // (2 * FAST_UNROLL)
    lax.fori_loop(0, zero + (n_bodies - 1), lambda ib, carry: body(ib, False) or carry, 0)
    body(n_bodies - 1, True)

    @pl.when(jnp.logical_not(_sums_in_range(acc_ref[:, sum_row:sum_row + 1, :])))
    def _():
        acc_ref[...] = jnp.zeros_like(acc_ref)

        blocks = sorted({(c, b0) for c, b0, _ in units})

        def step(j, ms):
            ms = dict(zip(blocks, ms))
            for c, b0, st in units:
                ms[c, b0] = _softmax_step(scores(j, c, b0, st), ms[c, b0], v_strip(j, st),
                                          acc_ref, (c, slice(None), cols(b0)))
            return tuple(ms[blk] for blk in blocks)

        lax.fori_loop(0, nk, step, (jnp.full((1, QBLK), -jnp.inf, F32),) * len(blocks))


def _strip_rows(j, strip):
    return pl.ds(pl.multiple_of(j * CHUNK + strip * KSTRIP, KSTRIP), KSTRIP)


def _strip_cols(strip):
    return slice(strip * KSTRIP, (strip + 1) * KSTRIP)


def _traced_zero():
    return jnp.minimum(pl.program_id(0), 0)


def _attn_a_kernel(q0_ref, q1_ref, k_ref, v_ref, o_ref, acc_ref, p_ref):
    q_refs = (q0_ref, q1_ref)
    _attention_core(
        q_of=lambda c, cols, st: q_refs[st][0, 0, :, cols],
        k_strip=lambda j, st: k_ref[0, _strip_rows(j, st), :],
        v_strip=lambda j, st: v_ref[0, 0, j, :, _strip_cols(st)],
        bias_of=None, acc_ref=acc_ref, p_ref=p_ref, nk=v_ref.shape[2],
        zero=_traced_zero())
    acc = acc_ref[0]
    o_ref[0, 0] = (acc[0:HEAD_DIM] / acc[HEAD_DIM:HEAD_DIM + 1]).astype(BF16)


def _attn_a(qa, ka, va):
    bsz, _, _, seq = qa.shape
    ns = seq // CHUNK
    return pl.pallas_call(
        _attn_a_kernel,
        grid=(bsz, A_Q_HEADS, seq // TQ_A),
        in_specs=[
            pl.BlockSpec((1, 1, 2 * HEAD_DIM, TQ_A), lambda b, h, q: (b, h, 0, q)),
            pl.BlockSpec((1, 1, 2 * HEAD_DIM, TQ_A), lambda b, h, q: (b, h, 0, q)),
            pl.BlockSpec((1, seq, 2 * HEAD_DIM), lambda b, h, q: (b, 0, 0)),
            pl.BlockSpec((1, 1, ns, HEAD_DIM + ONES_ROWS, CHUNK),
                         lambda b, h, q: (b, h // A_GROUP, 0, 0, 0)),
        ],
        out_specs=pl.BlockSpec((1, 1, HEAD_DIM, TQ_A), lambda b, h, q: (b, h, 0, q)),
        out_shape=jax.ShapeDtypeStruct((bsz, A_Q_HEADS, HEAD_DIM, seq), BF16),
        scratch_shapes=[pltpu.VMEM((1, HEAD_DIM + ONES_ROWS, TQ_A), F32),
                        pltpu.VMEM((2, 1, CHUNK, TQ_A), BF16)],
        compiler_params=pltpu.CompilerParams(
            dimension_semantics=("arbitrary", "arbitrary", "arbitrary"),
            vmem_limit_bytes=VMEM_LIMIT),
        name="attn_a",
    )(qa, qa, ka, va)


def _attn_b_kernel(q0_ref, q1_ref, k_ref, v_ref, bias_ref, lq1_ref, lk1_ref, lq2_ref,
                   lk2_ref, g_ref, o_ref, acc_ref, p_ref, *, lambda_init):
    q_refs = (q0_ref, q1_ref)
    qi = pl.program_id(2)
    far = BIAS_KINDS // 2
    first_block = qi * (TQ_B // CHUNK)
    _attention_core(
        q_of=lambda c, cols, st: q_refs[st][0, 0, c, :, cols],
        k_strip=lambda j, st: k_ref[0, 0, _strip_rows(j, st), :],
        v_strip=lambda j, st: v_ref[0, 0, j, :, _strip_cols(st)],
        bias_of=lambda j, b0, st: bias_ref[
            0, jnp.clip(j - (first_block + b0 // CHUNK), -far, far) + far, _strip_cols(st), :],
        acc_ref=acc_ref, p_ref=p_ref, nk=v_ref.shape[2], zero=_traced_zero())

    lam = (jnp.exp(jnp.sum(lq1_ref[...] * lk1_ref[...], keepdims=True))
           - jnp.exp(jnp.sum(lq2_ref[...] * lk2_ref[...], keepdims=True))
           + lambda_init)
    a0, a1 = acc_ref[0], acc_ref[1]
    o = (a0[0:B_V_DIM] / a0[B_V_DIM:B_V_DIM + 1]
         - lam * (a1[0:B_V_DIM] / a1[B_V_DIM:B_V_DIM + 1]))
    msq = jnp.mean(o * o, axis=0, keepdims=True)
    o_ref[0, 0] = (o * lax.rsqrt(msq + EPS) * g_ref[...]).astype(BF16)


def _attn_b(qb, kb, vb, bias, lq1, lk1, lq2, lk2, gsub, lambda_init):
    bsz, _, _, _, seq = qb.shape
    ns = seq // CHUNK
    vec = pl.BlockSpec((1, HEAD_DIM), lambda b, h, q: (0, 0))
    return pl.pallas_call(
        functools.partial(_attn_b_kernel, lambda_init=lambda_init),
        grid=(bsz, B_HEADS, seq // TQ_B),
        in_specs=[
            pl.BlockSpec((1, 1, 2, 2 * HEAD_DIM, TQ_B), lambda b, h, q: (b, h, 0, 0, q)),
            pl.BlockSpec((1, 1, 2, 2 * HEAD_DIM, TQ_B), lambda b, h, q: (b, h, 0, 0, q)),
            pl.BlockSpec((1, 1, seq, 2 * HEAD_DIM), lambda b, h, q: (b, h, 0, 0)),
            pl.BlockSpec((1, 1, ns, B_V_DIM + ONES_ROWS, CHUNK),
                         lambda b, h, q: (b, h, 0, 0, 0)),
            pl.BlockSpec((1, BIAS_KINDS, CHUNK, CHUNK), lambda b, h, q: (h, 0, 0, 0)),
            vec, vec, vec, vec,
            pl.BlockSpec((B_V_DIM, TQ_B), lambda b, h, q: (0, 0)),
        ],
        out_specs=pl.BlockSpec((1, 1, B_V_DIM, TQ_B), lambda b, h, q: (b, h, 0, q)),
        out_shape=jax.ShapeDtypeStruct((bsz, B_HEADS, B_V_DIM, seq), BF16),
        scratch_shapes=[pltpu.VMEM((2, B_V_DIM + ONES_ROWS, TQ_B), F32),
                        pltpu.VMEM((2, 2, CHUNK, TQ_B), BF16)],
        compiler_params=pltpu.CompilerParams(
            dimension_semantics=("arbitrary", "arbitrary", "arbitrary"),
            vmem_limit_bytes=VMEM_LIMIT),
        name="attn_b",
    )(qb, qb, kb, vb, bias, lq1, lk1, lq2, lk2, gsub)


def _out_mlp_kernel(x_ref, oa_ref, ob_ref, wo_ref, g_ref, wu_ref, wd_ref, y_ref):
    tn = (((0,), (0,)), ((), ()))
    attn = lax.dot_general(oa_ref[0], wo_ref[0:A_Q, :], tn, preferred_element_type=F32)
    attn = attn + lax.dot_general(ob_ref[0], wo_ref[A_Q:A_Q + B_V, :], tn,
                                  preferred_element_type=F32)
    x1 = x_ref[0] + attn
    ms = jnp.mean(x1 * x1, axis=-1, keepdims=True)
    h = (x1 * lax.rsqrt(ms + EPS) * g_ref[...]).astype(BF16)
    acc = x1
    for c in range(wu_ref.shape[1] // MLP_FCHUNK):
        cols = slice(c * MLP_FCHUNK, (c + 1) * MLP_FCHUNK)
        u = jnp.dot(h, wu_ref[:, cols], preferred_element_type=F32)
        u = jnp.square(jnp.maximum(u, 0.0)).astype(BF16)
        acc = acc + jnp.dot(u, wd_ref[cols, :], preferred_element_type=F32)
    y_ref[0] = acc


def _out_mlp(x, oa, ob, w_out, g2, w_up, w_down):
    bsz, seq, d = x.shape
    d_ff = w_up.shape[1]
    resident = lambda shape: pl.BlockSpec(shape, lambda b, s: (0,) * len(shape),
                                          pipeline_mode=pl.Buffered(1))
    return pl.pallas_call(
        _out_mlp_kernel,
        grid=(bsz, seq // MLP_ROWS),
        in_specs=[
            pl.BlockSpec((1, MLP_ROWS, d), lambda b, s: (b, s, 0)),
            pl.BlockSpec((1, A_Q, MLP_ROWS), lambda b, s: (b, 0, s)),
            pl.BlockSpec((1, B_V, MLP_ROWS), lambda b, s: (b, 0, s)),
            resident((A_Q + B_V, d)),
            resident((1, d)),
            resident((d, d_ff)),
            resident((d_ff, d)),
        ],
        out_specs=pl.BlockSpec((1, MLP_ROWS, d), lambda b, s: (b, s, 0)),
        out_shape=jax.ShapeDtypeStruct((bsz, seq, d), F32),
        compiler_params=pltpu.CompilerParams(
            dimension_semantics=("arbitrary", "arbitrary"),
            vmem_limit_bytes=VMEM_LIMIT),
        name="out_mlp",
    )(x, oa, ob, w_out, g2, w_up, w_down)


def _rope_tables_t(seq):
    pos = jnp.arange(seq)
    row = (pos // GRID_W).astype(F32)
    col = (pos % GRID_W).astype(F32)
    half = HEAD_DIM // 2
    inv_freq = 1.0 / (ROPE_THETA ** (jnp.arange(0, half, 2, dtype=F32) / half))
    ang_r = inv_freq[:, None] * row[None, :]
    ang_c = inv_freq[:, None] * col[None, :]
    return jnp.cos(ang_r), jnp.sin(ang_r), jnp.cos(ang_c), jnp.sin(ang_c)


def kernel(x, attn_norm_g, w_in, a_q_norm_g, a_k_norm_g, b_q_norm_g, b_k_norm_g,
           lambda_q1, lambda_k1, lambda_q2, lambda_k2, b_subln_g, rel_bias,
           w_out, mlp_norm_g, w_up, w_down):
    bsz, seq, d = x.shape
    depth = w_in.shape[0]
    assert seq % CHUNK == 0 and seq % TQ_A == 0 and seq % MLP_ROWS == 0
    assert w_in.shape[2] == IN_WIDTH and w_up.shape[2] % MLP_FCHUNK == 0
    cr, sr, cc, sc = _rope_tables_t(seq)
    qscale = HEAD_DIM ** -0.5 * LOG2E
    lanes = lambda v, n: jnp.broadcast_to(v.astype(F32)[:, None], (v.shape[0], n))
    bias = _bias_tiles(rel_bias.astype(F32).T)
    for l in range(depth):
        lambda_init = 0.8 - 0.6 * math.exp(-0.3 * l)
        qa, ka, va, qb, kb, vb = _in_proj(
            x, attn_norm_g[l].astype(F32)[None, :], w_in[l].T.astype(BF16),
            lanes(a_q_norm_g[l] * qscale, CHUNK), lanes(a_k_norm_g[l], CHUNK),
            lanes(b_q_norm_g[l] * qscale, CHUNK), lanes(b_k_norm_g[l], CHUNK),
            cr, sr, cc, sc)
        oa = _attn_a(qa, ka, va)
        ob = _attn_b(qb, kb, vb, bias,
                     lambda_q1[l].astype(F32)[None, :], lambda_k1[l].astype(F32)[None, :],
                     lambda_q2[l].astype(F32)[None, :], lambda_k2[l].astype(F32)[None, :],
                     lanes(b_subln_g[l] * (1.0 - lambda_init), TQ_B), lambda_init)
        x = _out_mlp(x, oa.reshape(bsz, A_Q, seq), ob.reshape(bsz, B_V, seq),
                     w_out[l].astype(BF16), mlp_norm_g[l].astype(F32)[None, :],
                     w_up[l].astype(BF16), w_down[l].astype(BF16))
    return x
```

```python
import functools
import math

import jax
import jax.numpy as jnp
from jax import lax
from jax.experimental import pallas as pl
from jax.experimental.pallas import tpu as pltpu

HEAD_DIM = 64
A_Q_HEADS = 8
A_KV_HEADS = 2
A_GROUP = A_Q_HEADS // A_KV_HEADS
B_HEADS = 4
B_V_DIM = 2 * HEAD_DIM
A_Q = A_Q_HEADS * HEAD_DIM
A_KV = A_KV_HEADS * HEAD_DIM
B_QK = B_HEADS * 2 * HEAD_DIM
B_V = B_HEADS * B_V_DIM
GRID_W = 64
ROPE_THETA = 10000.0
REL_BUCKETS = 32
EPS = 1e-6
LOG2E = 1.4426950408889634
T5_BUCKET_STARTS = tuple(range(9)) + (12, 16, 23, 32, 46, 64, 91)

OFF_QA = 0
OFF_KA = OFF_QA + A_Q
OFF_VA = OFF_KA + A_KV
OFF_QB = OFF_VA + A_KV
OFF_KB = OFF_QB + B_QK
OFF_VB = OFF_KB + B_QK
IN_WIDTH = OFF_VB + B_V

BF16_SUBLANES = 16
ONES_ROWS = BF16_SUBLANES
CHUNK = 512
TQ_A = 2048
TQ_B = 1024
BIAS_KINDS = 5
MLP_ROWS = 512
MLP_FCHUNK = 1024
VMEM_LIMIT = 48 * 1024 * 1024
FAST_UNROLL = 2
KSTRIP_A = 256
KSTRIP_B = CHUNK
QBLK = CHUNK
SUM_MIN = 2.0 ** -64
SUM_MAX = 2.0 ** 64

BF16 = jnp.bfloat16
F32 = jnp.float32


def _bias_tile_kernel(tbl_ref, out_ref):
    h = pl.program_id(0)
    d = pl.program_id(1) - BIAS_KINDS // 2
    kk = lax.broadcasted_iota(jnp.int32, (CHUNK, CHUNK), 0)
    qq = lax.broadcasted_iota(jnp.int32, (CHUNK, CHUNK), 1)
    rel = d * CHUNK + kk - qq
    n = jnp.abs(rel)

    def side(first_bucket):
        v = jnp.full((CHUNK, CHUNK), tbl_ref[h, first_bucket] * LOG2E, F32)
        for b, start in enumerate(T5_BUCKET_STARTS[1:], 1):
            v = jnp.where(n >= start, tbl_ref[h, first_bucket + b] * LOG2E, v)
        return v

    @pl.when(d < 0)
    def _():
        out_ref[0, 0] = side(0)

    @pl.when(d > 0)
    def _():
        out_ref[0, 0] = side(REL_BUCKETS // 2)

    @pl.when(d == 0)
    def _():
        out_ref[0, 0] = jnp.where(rel > 0, side(REL_BUCKETS // 2), side(0))


def _bias_tiles(rel_bias_t):
    return pl.pallas_call(
        _bias_tile_kernel,
        grid=(B_HEADS, BIAS_KINDS),
        in_specs=[pl.BlockSpec(memory_space=pltpu.SMEM)],
        out_specs=pl.BlockSpec((1, 1, CHUNK, CHUNK), lambda h, d: (h, d, 0, 0)),
        out_shape=jax.ShapeDtypeStruct((B_HEADS, BIAS_KINDS, CHUNK, CHUNK), F32),
        name="bias_tiles",
    )(rel_bias_t)


def _in_proj_kernel(x_ref, g_ref, w_ref, gqa_ref, gka_ref, gqb_ref, gkb_ref,
                    cr_ref, sr_ref, cc_ref, sc_ref,
                    qa_ref, ka_ref, va_ref, qb_ref, kb_ref, vb_ref, pt_ref):
    x = x_ref[0]
    ms = jnp.mean(x * x, axis=-1, keepdims=True)
    h = (x * lax.rsqrt(ms + EPS) * g_ref[...]).astype(BF16)
    cr, sr, cc, sc = cr_ref[...], sr_ref[...], cc_ref[...], sc_ref[...]
    quarter = HEAD_DIM // 4

    def project(r0, n):
        pt_ref[r0:r0 + n, :] = lax.dot_general(
            w_ref[r0:r0 + n, :], h, (((1,), (1,)), ((), ())), preferred_element_type=F32)

    def norm(r0, gain_ref):
        t = pt_ref[r0:r0 + HEAD_DIM, :]
        m = jnp.mean(t * t, axis=0, keepdims=True)
        return t * lax.rsqrt(m + EPS) * gain_ref[...]

    def rope(y):
        x1r, x2r = y[0:quarter], y[quarter:2 * quarter]
        x1c, x2c = y[2 * quarter:3 * quarter], y[3 * quarter:]
        return jnp.concatenate(
            [x1r * cr - x2r * sr, x2r * cr + x1r * sr,
             x1c * cc - x2c * sc, x2c * cc + x1c * sc], axis=0)

    zeros = jnp.zeros((HEAD_DIM, CHUNK), BF16)
    ones = jnp.ones((ONES_ROWS, CHUNK), BF16)

    def finish_qa():
        for hh in range(A_Q_HEADS):
            q = rope(norm(OFF_QA + hh * HEAD_DIM, gqa_ref)).astype(BF16)
            slot = hh // A_GROUP
            qa_ref[0, hh, slot * HEAD_DIM:(slot + 1) * HEAD_DIM, :] = q
            qa_ref[0, hh, (1 - slot) * HEAD_DIM:(2 - slot) * HEAD_DIM, :] = zeros

    def finish_kva():
        kt = jnp.concatenate([rope(norm(OFF_KA + kv * HEAD_DIM, gka_ref))
                              for kv in range(A_KV_HEADS)], axis=0)
        ka_ref[0] = kt.T.astype(BF16)
        for kv in range(A_KV_HEADS):
            r0 = OFF_VA + kv * HEAD_DIM
            va_ref[0, kv, 0, 0:HEAD_DIM, :] = pt_ref[r0:r0 + HEAD_DIM, :].astype(BF16)
            va_ref[0, kv, 0, HEAD_DIM:HEAD_DIM + ONES_ROWS, :] = ones

    def finish_qb():
        for hb in range(B_HEADS):
            for c in range(2):
                q = norm(OFF_QB + (hb * 2 + c) * HEAD_DIM, gqb_ref).astype(BF16)
                qb_ref[0, hb, c, c * HEAD_DIM:(c + 1) * HEAD_DIM, :] = q
                qb_ref[0, hb, c, (1 - c) * HEAD_DIM:(2 - c) * HEAD_DIM, :] = zeros

    def finish_kb():
        for hb in range(B_HEADS):
            kt = jnp.concatenate([norm(OFF_KB + (hb * 2 + c) * HEAD_DIM, gkb_ref)
                                  for c in range(2)], axis=0)
            kb_ref[0, hb] = kt.T.astype(BF16)

    def finish_vb():
        for hb in range(B_HEADS):
            r0 = OFF_VB + hb * B_V_DIM
            vb_ref[0, hb, 0, 0:B_V_DIM, :] = pt_ref[r0:r0 + B_V_DIM, :].astype(BF16)
            vb_ref[0, hb, 0, B_V_DIM:B_V_DIM + ONES_ROWS, :] = ones

    project(OFF_QA, A_Q)
    project(OFF_KA, 2 * A_KV)
    finish_qa()
    project(OFF_QB, B_QK)
    finish_kva()
    project(OFF_KB, B_QK)
    finish_qb()
    project(OFF_VB, B_V)
    finish_kb()
    finish_vb()


def _in_proj(x, g, w_t, gqa, gka, gqb, gkb, cr, sr, cc, sc):
    bsz, seq, d = x.shape
    ns = seq // CHUNK
    const = lambda shape: pl.BlockSpec(shape, lambda b, s: (0,) * len(shape))
    tab = pl.BlockSpec((HEAD_DIM // 4, CHUNK), lambda b, s: (0, s))
    out_shape = (
        jax.ShapeDtypeStruct((bsz, A_Q_HEADS, 2 * HEAD_DIM, seq), BF16),
        jax.ShapeDtypeStruct((bsz, seq, 2 * HEAD_DIM), BF16),
        jax.ShapeDtypeStruct((bsz, A_KV_HEADS, ns, HEAD_DIM + ONES_ROWS, CHUNK), BF16),
        jax.ShapeDtypeStruct((bsz, B_HEADS, 2, 2 * HEAD_DIM, seq), BF16),
        jax.ShapeDtypeStruct((bsz, B_HEADS, seq, 2 * HEAD_DIM), BF16),
        jax.ShapeDtypeStruct((bsz, B_HEADS, ns, B_V_DIM + ONES_ROWS, CHUNK), BF16),
    )
    out_specs = (
        pl.BlockSpec((1, A_Q_HEADS, 2 * HEAD_DIM, CHUNK), lambda b, s: (b, 0, 0, s)),
        pl.BlockSpec((1, CHUNK, 2 * HEAD_DIM), lambda b, s: (b, s, 0)),
        pl.BlockSpec((1, A_KV_HEADS, 1, HEAD_DIM + ONES_ROWS, CHUNK),
                     lambda b, s: (b, 0, s, 0, 0)),
        pl.BlockSpec((1, B_HEADS, 2, 2 * HEAD_DIM, CHUNK), lambda b, s: (b, 0, 0, 0, s)),
        pl.BlockSpec((1, B_HEADS, CHUNK, 2 * HEAD_DIM), lambda b, s: (b, 0, s, 0)),
        pl.BlockSpec((1, B_HEADS, 1, B_V_DIM + ONES_ROWS, CHUNK),
                     lambda b, s: (b, 0, s, 0, 0)),
    )
    return pl.pallas_call(
        _in_proj_kernel,
        grid=(bsz, ns),
        in_specs=[
            pl.BlockSpec((1, CHUNK, d), lambda b, s: (b, s, 0)),
            const((1, d)),
            const((IN_WIDTH, d)),
            const((HEAD_DIM, CHUNK)), const((HEAD_DIM, CHUNK)),
            const((HEAD_DIM, CHUNK)), const((HEAD_DIM, CHUNK)),
            tab, tab, tab, tab,
        ],
        out_specs=out_specs,
        out_shape=out_shape,
        scratch_shapes=[pltpu.VMEM((IN_WIDTH, CHUNK), F32)],
        compiler_params=pltpu.CompilerParams(
            dimension_semantics=("arbitrary", "arbitrary"),
            vmem_limit_bytes=VMEM_LIMIT),
        name="in_proj",
    )(x, g, w_t, gqa, gka, gqb, gkb, cr, sr, cc, sc)


def _softmax_step(s, m, v_t, acc_ref, idx):
    m_new = jnp.maximum(m, jnp.max(s, axis=0, keepdims=True))
    alpha = jnp.exp2(m - m_new)
    p = jnp.exp2(s - m_new).astype(BF16)
    pv = jnp.dot(v_t, p, preferred_element_type=F32)
    acc_ref[idx] = alpha * acc_ref[idx] + pv
    return m_new


def _sums_in_range(sums):
    good = jnp.logical_and(sums >= SUM_MIN, sums <= SUM_MAX)
    return jnp.min(jnp.where(good, 1.0, 0.0)) > 0.5


def _attention_core(q_of, k_strip, v_strip, bias_of, acc_ref, p_ref, nk, zero, kstrip):
    n_maps, rows, tq = acc_ref.shape
    sum_row = rows - ONES_ROWS
    units = [(b0, st) for b0 in range(0, tq, QBLK) for st in range(CHUNK // kstrip)]
    cols = lambda b0: slice(b0, b0 + QBLK)
    keys = lambda st: slice(st * kstrip, (st + 1) * kstrip)

    def scores(j, b0, st):
        k = k_strip(j, st)
        bias = None if bias_of is None else bias_of(j, b0, st)
        out = []
        for c in range(n_maps):
            s = jnp.dot(k, q_of(c, cols(b0), st), preferred_element_type=F32)
            out.append(s if bias is None else s + bias)
        return out

    def produce(j, slot, b0, st):
        for c, s in enumerate(scores(j, b0, st)):
            p_ref[slot, c, keys(st), cols(b0)] = jnp.exp2(s).astype(BF16)

    def consume(j, slot, b0, st):
        v_t = v_strip(j, st)
        for c in range(n_maps):
            acc_ref[c, :, cols(b0)] += jnp.dot(v_t, p_ref[slot, c, keys(st), cols(b0)],
                                               preferred_element_type=F32)

    acc_ref[...] = jnp.zeros_like(acc_ref)

    for unit in units:
        produce(0, 0, *unit)

    def body(ib, final):
        for t in range(2 * FAST_UNROLL):
            j = 2 * FAST_UNROLL * ib + t
            for unit in units:
                if not (final and t == 2 * FAST_UNROLL - 1):
                    produce(j + 1, 1 - t % 2, *unit)
                consume(j, t % 2, *unit)

    n_bodies = nk // (2 * FAST_UNROLL)
    lax.fori_loop(0, zero + (n_bodies - 1), lambda ib, carry: body(ib, False) or carry, 0)
    body(n_bodies - 1, True)

    @pl.when(jnp.logical_not(_sums_in_range(acc_ref[:, sum_row:sum_row + 1, :])))
    def _():
        acc_ref[...] = jnp.zeros_like(acc_ref)

        blocks = sorted({(c, b0) for b0, _ in units for c in range(n_maps)})

        def step(j, ms):
            ms = dict(zip(blocks, ms))
            for b0, st in units:
                for c, s in enumerate(scores(j, b0, st)):
                    ms[c, b0] = _softmax_step(s, ms[c, b0], v_strip(j, st), acc_ref,
                                              (c, slice(None), cols(b0)))
            return tuple(ms[blk] for blk in blocks)

        lax.fori_loop(0, nk, step, (jnp.full((1, QBLK), -jnp.inf, F32),) * len(blocks))


def _strip_rows(j, strip, kstrip):
    return pl.ds(pl.multiple_of(j * CHUNK + strip * kstrip, kstrip), kstrip)


def _strip_cols(strip, kstrip):
    return slice(strip * kstrip, (strip + 1) * kstrip)


def _traced_zero():
    return jnp.minimum(pl.program_id(0), 0)


def _attn_a_kernel(q0_ref, q1_ref, k_ref, v_ref, o_ref, acc_ref, p_ref):
    q_refs = (q0_ref, q1_ref)
    _attention_core(
        q_of=lambda c, cols, st: q_refs[st][0, 0, :, cols],
        k_strip=lambda j, st: k_ref[0, _strip_rows(j, st, KSTRIP_A), :],
        v_strip=lambda j, st: v_ref[0, 0, j, :, _strip_cols(st, KSTRIP_A)],
        bias_of=None, acc_ref=acc_ref, p_ref=p_ref, nk=v_ref.shape[2],
        zero=_traced_zero(), kstrip=KSTRIP_A)
    acc = acc_ref[0]
    o_ref[0, 0] = (acc[0:HEAD_DIM] / acc[HEAD_DIM:HEAD_DIM + 1]).astype(BF16)


def _attn_a(qa, ka, va):
    bsz, _, _, seq = qa.shape
    ns = seq // CHUNK
    return pl.pallas_call(
        _attn_a_kernel,
        grid=(bsz, A_Q_HEADS, seq // TQ_A),
        in_specs=[
            pl.BlockSpec((1, 1, 2 * HEAD_DIM, TQ_A), lambda b, h, q: (b, h, 0, q)),
            pl.BlockSpec((1, 1, 2 * HEAD_DIM, TQ_A), lambda b, h, q: (b, h, 0, q)),
            pl.BlockSpec((1, seq, 2 * HEAD_DIM), lambda b, h, q: (b, 0, 0)),
            pl.BlockSpec((1, 1, ns, HEAD_DIM + ONES_ROWS, CHUNK),
                         lambda b, h, q: (b, h // A_GROUP, 0, 0, 0)),
        ],
        out_specs=pl.BlockSpec((1, 1, HEAD_DIM, TQ_A), lambda b, h, q: (b, h, 0, q)),
        out_shape=jax.ShapeDtypeStruct((bsz, A_Q_HEADS, HEAD_DIM, seq), BF16),
        scratch_shapes=[pltpu.VMEM((1, HEAD_DIM + ONES_ROWS, TQ_A), F32),
                        pltpu.VMEM((2, 1, CHUNK, TQ_A), BF16)],
        compiler_params=pltpu.CompilerParams(
            dimension_semantics=("arbitrary", "arbitrary", "arbitrary"),
            vmem_limit_bytes=VMEM_LIMIT),
        name="attn_a",
    )(qa, qa, ka, va)


def _attn_b_kernel(q_ref, k_ref, v_ref, bias_ref, lq1_ref, lk1_ref, lq2_ref,
                   lk2_ref, g_ref, o_ref, acc_ref, p_ref, *, lambda_init):
    qi = pl.program_id(2)
    far = BIAS_KINDS // 2
    first_block = qi * (TQ_B // CHUNK)
    _attention_core(
        q_of=lambda c, cols, st: q_ref[0, 0, c, :, cols],
        k_strip=lambda j, st: k_ref[0, 0, _strip_rows(j, st, KSTRIP_B), :],
        v_strip=lambda j, st: v_ref[0, 0, j, :, _strip_cols(st, KSTRIP_B)],
        bias_of=lambda j, b0, st: bias_ref[
            0, jnp.clip(j - (first_block + b0 // CHUNK), -far, far) + far,
            _strip_cols(st, KSTRIP_B), :],
        acc_ref=acc_ref, p_ref=p_ref, nk=v_ref.shape[2], zero=_traced_zero(),
        kstrip=KSTRIP_B)

    lam = (jnp.exp(jnp.sum(lq1_ref[...] * lk1_ref[...], keepdims=True))
           - jnp.exp(jnp.sum(lq2_ref[...] * lk2_ref[...], keepdims=True))
           + lambda_init)
    a0, a1 = acc_ref[0], acc_ref[1]
    o = (a0[0:B_V_DIM] / a0[B_V_DIM:B_V_DIM + 1]
         - lam * (a1[0:B_V_DIM] / a1[B_V_DIM:B_V_DIM + 1]))
    msq = jnp.mean(o * o, axis=0, keepdims=True)
    o_ref[0, 0] = (o * lax.rsqrt(msq + EPS) * g_ref[...]).astype(BF16)


def _attn_b(qb, kb, vb, bias, lq1, lk1, lq2, lk2, gsub, lambda_init):
    bsz, _, _, _, seq = qb.shape
    ns = seq // CHUNK
    vec = pl.BlockSpec((1, HEAD_DIM), lambda b, h, q: (0, 0))
    return pl.pallas_call(
        functools.partial(_attn_b_kernel, lambda_init=lambda_init),
        grid=(bsz, B_HEADS, seq // TQ_B),
        in_specs=[
            pl.BlockSpec((1, 1, 2, 2 * HEAD_DIM, TQ_B), lambda b, h, q: (b, h, 0, 0, q)),
            pl.BlockSpec((1, 1, seq, 2 * HEAD_DIM), lambda b, h, q: (b, h, 0, 0)),
            pl.BlockSpec((1, 1, ns, B_V_DIM + ONES_ROWS, CHUNK),
                         lambda b, h, q: (b, h, 0, 0, 0)),
            pl.BlockSpec((1, BIAS_KINDS, CHUNK, CHUNK), lambda b, h, q: (h, 0, 0, 0)),
            vec, vec, vec, vec,
            pl.BlockSpec((B_V_DIM, TQ_B), lambda b, h, q: (0, 0)),
        ],
        out_specs=pl.BlockSpec((1, 1, B_V_DIM, TQ_B), lambda b, h, q: (b, h, 0, q)),
        out_shape=jax.ShapeDtypeStruct((bsz, B_HEADS, B_V_DIM, seq), BF16),
        scratch_shapes=[pltpu.VMEM((2, B_V_DIM + ONES_ROWS, TQ_B), F32),
                        pltpu.VMEM((2, 2, CHUNK, TQ_B), BF16)],
        compiler_params=pltpu.CompilerParams(
            dimension_semantics=("arbitrary", "arbitrary", "arbitrary"),
            vmem_limit_bytes=VMEM_LIMIT),
        name="attn_b",
    )(qb, kb, vb, bias, lq1, lk1, lq2, lk2, gsub)


def _out_mlp_kernel(x_ref, oa_ref, ob_ref, wo_ref, g_ref, wu_ref, wd_ref, y_ref):
    tn = (((0,), (0,)), ((), ()))
    attn = lax.dot_general(oa_ref[0], wo_ref[0:A_Q, :], tn, preferred_element_type=F32)
    attn = attn + lax.dot_general(ob_ref[0], wo_ref[A_Q:A_Q + B_V, :], tn,
                                  preferred_element_type=F32)
    x1 = x_ref[0] + attn
    ms = jnp.mean(x1 * x1, axis=-1, keepdims=True)
    h = (x1 * lax.rsqrt(ms + EPS) * g_ref[...]).astype(BF16)
    acc = x1
    for c in range(wu_ref.shape[1] // MLP_FCHUNK):
        cols = slice(c * MLP_FCHUNK, (c + 1) * MLP_FCHUNK)
        u = jnp.dot(h, wu_ref[:, cols], preferred_element_type=F32)
        u = jnp.square(jnp.maximum(u, 0.0)).astype(BF16)
        acc = acc + jnp.dot(u, wd_ref[cols, :], preferred_element_type=F32)
    y_ref[0] = acc


def _out_mlp(x, oa, ob, w_out, g2, w_up, w_down):
    bsz, seq, d = x.shape
    d_ff = w_up.shape[1]
    resident = lambda shape: pl.BlockSpec(shape, lambda b, s: (0,) * len(shape),
                                          pipeline_mode=pl.Buffered(1))
    return pl.pallas_call(
        _out_mlp_kernel,
        grid=(bsz, seq // MLP_ROWS),
        in_specs=[
            pl.BlockSpec((1, MLP_ROWS, d), lambda b, s: (b, s, 0)),
            pl.BlockSpec((1, A_Q, MLP_ROWS), lambda b, s: (b, 0, s)),
            pl.BlockSpec((1, B_V, MLP_ROWS), lambda b, s: (b, 0, s)),
            resident((A_Q + B_V, d)),
            resident((1, d)),
            resident((d, d_ff)),
            resident((d_ff, d)),
        ],
        out_specs=pl.BlockSpec((1, MLP_ROWS, d), lambda b, s: (b, s, 0)),
        out_shape=jax.ShapeDtypeStruct((bsz, seq, d), F32),
        compiler_params=pltpu.CompilerParams(
            dimension_semantics=("arbitrary", "arbitrary"),
            vmem_limit_bytes=VMEM_LIMIT),
        name="out_mlp",
    )(x, oa, ob, w_out, g2, w_up, w_down)


def _rope_tables_t(seq):
    pos = jnp.arange(seq)
    row = (pos // GRID_W).astype(F32)
    col = (pos % GRID_W).astype(F32)
    half = HEAD_DIM // 2
    inv_freq = 1.0 / (ROPE_THETA ** (jnp.arange(0, half, 2, dtype=F32) / half))
    ang_r = inv_freq[:, None] * row[None, :]
    ang_c = inv_freq[:, None] * col[None, :]
    return jnp.cos(ang_r), jnp.sin(ang_r), jnp.cos(ang_c), jnp.sin(ang_c)


def kernel(x, attn_norm_g, w_in, a_q_norm_g, a_k_norm_g, b_q_norm_g, b_k_norm_g,
           lambda_q1, lambda_k1, lambda_q2, lambda_k2, b_subln_g, rel_bias,
           w_out, mlp_norm_g, w_up, w_down):
    bsz, seq, d = x.shape
    depth = w_in.shape[0]
    assert seq % CHUNK == 0 and seq % TQ_A == 0 and seq % MLP_ROWS == 0
    assert w_in.shape[2] == IN_WIDTH and w_up.shape[2] % MLP_FCHUNK == 0
    cr, sr, cc, sc = _rope_tables_t(seq)
    qscale = HEAD_DIM ** -0.5 * LOG2E
    lanes = lambda v, n: jnp.broadcast_to(v.astype(F32)[:, None], (v.shape[0], n))
    bias = _bias_tiles(rel_bias.astype(F32).T)
    for l in range(depth):
        lambda_init = 0.8 - 0.6 * math.exp(-0.3 * l)
        qa, ka, va, qb, kb, vb = _in_proj(
            x, attn_norm_g[l].astype(F32)[None, :], w_in[l].T.astype(BF16),
            lanes(a_q_norm_g[l] * qscale, CHUNK), lanes(a_k_norm_g[l], CHUNK),
            lanes(b_q_norm_g[l] * qscale, CHUNK), lanes(b_k_norm_g[l], CHUNK),
            cr, sr, cc, sc)
        oa = _attn_a(qa, ka, va)
        ob = _attn_b(qb, kb, vb, bias,
                     lambda_q1[l].astype(F32)[None, :], lambda_k1[l].astype(F32)[None, :],
                     lambda_q2[l].astype(F32)[None, :], lambda_k2[l].astype(F32)[None, :],
                     lanes(b_subln_g[l] * (1.0 - lambda_init), TQ_B), lambda_init)
        x = _out_mlp(x, oa.reshape(bsz, A_Q, seq), ob.reshape(bsz, B_V, seq),
                     w_out[l].astype(BF16), mlp_norm_g[l].astype(F32)[None, :],
                     w_up[l].astype(BF16), w_down[l].astype(BF16))
    return x
```

```python
import functools
import math

import jax
import jax.numpy as jnp
from jax import lax
from jax.experimental import pallas as pl
from jax.experimental.pallas import tpu as pltpu

HEAD_DIM = 64
A_Q_HEADS = 8
A_KV_HEADS = 2
A_GROUP = A_Q_HEADS // A_KV_HEADS
B_HEADS = 4
B_V_DIM = 2 * HEAD_DIM
A_Q = A_Q_HEADS * HEAD_DIM
A_KV = A_KV_HEADS * HEAD_DIM
B_QK = B_HEADS * 2 * HEAD_DIM
B_V = B_HEADS * B_V_DIM
GRID_W = 64
ROPE_THETA = 10000.0
REL_BUCKETS = 32
EPS = 1e-6
LOG2E = 1.4426950408889634
T5_BUCKET_STARTS = tuple(range(9)) + (12, 16, 23, 32, 46, 64, 91)

OFF_QA = 0
OFF_KA = OFF_QA + A_Q
OFF_VA = OFF_KA + A_KV
OFF_QB = OFF_VA + A_KV
OFF_KB = OFF_QB + B_QK
OFF_VB = OFF_KB + B_QK
IN_WIDTH = OFF_VB + B_V

BF16_SUBLANES = 16
ONES_ROWS = BF16_SUBLANES
CHUNK = 512
TQ_A = 4096
TQ_B = 2048
BIAS_KINDS = 5
MLP_ROWS = 512
MLP_FCHUNK = 1024
VMEM_LIMIT = 48 * 1024 * 1024
FAST_UNROLL = 1
KSTRIP_A = 256
KSTRIP_B = CHUNK
QBLK = CHUNK
SUM_MIN = 2.0 ** -64
SUM_MAX = 2.0 ** 64

BF16 = jnp.bfloat16
F32 = jnp.float32


def _bias_tile_kernel(tbl_ref, out_ref):
    h = pl.program_id(0)
    d = pl.program_id(1) - BIAS_KINDS // 2
    kk = lax.broadcasted_iota(jnp.int32, (CHUNK, CHUNK), 0)
    qq = lax.broadcasted_iota(jnp.int32, (CHUNK, CHUNK), 1)
    rel = d * CHUNK + kk - qq
    n = jnp.abs(rel)

    def side(first_bucket):
        v = jnp.full((CHUNK, CHUNK), tbl_ref[h, first_bucket] * LOG2E, F32)
        for b, start in enumerate(T5_BUCKET_STARTS[1:], 1):
            v = jnp.where(n >= start, tbl_ref[h, first_bucket + b] * LOG2E, v)
        return v

    @pl.when(d < 0)
    def _():
        out_ref[0, 0] = side(0)

    @pl.when(d > 0)
    def _():
        out_ref[0, 0] = side(REL_BUCKETS // 2)

    @pl.when(d == 0)
    def _():
        out_ref[0, 0] = jnp.where(rel > 0, side(REL_BUCKETS // 2), side(0))


def _bias_tiles(rel_bias_t):
    return pl.pallas_call(
        _bias_tile_kernel,
        grid=(B_HEADS, BIAS_KINDS),
        in_specs=[pl.BlockSpec(memory_space=pltpu.SMEM)],
        out_specs=pl.BlockSpec((1, 1, CHUNK, CHUNK), lambda h, d: (h, d, 0, 0)),
        out_shape=jax.ShapeDtypeStruct((B_HEADS, BIAS_KINDS, CHUNK, CHUNK), F32),
        name="bias_tiles",
    )(rel_bias_t)


def _in_proj_kernel(x_ref, g_ref, w_ref, gqa_ref, gka_ref, gqb_ref, gkb_ref,
                    cr_ref, sr_ref, cc_ref, sc_ref,
                    qa_ref, ka_ref, va_ref, qb_ref, kb_ref, vb_ref, pt_ref):
    x = x_ref[0]
    ms = jnp.mean(x * x, axis=-1, keepdims=True)
    h = (x * lax.rsqrt(ms + EPS) * g_ref[...]).astype(BF16)
    cr, sr, cc, sc = cr_ref[...], sr_ref[...], cc_ref[...], sc_ref[...]
    quarter = HEAD_DIM // 4

    def project(r0, n):
        pt_ref[r0:r0 + n, :] = lax.dot_general(
            w_ref[r0:r0 + n, :], h, (((1,), (1,)), ((), ())), preferred_element_type=F32)

    def norm(r0, gain_ref):
        t = pt_ref[r0:r0 + HEAD_DIM, :]
        m = jnp.mean(t * t, axis=0, keepdims=True)
        return t * lax.rsqrt(m + EPS) * gain_ref[...]

    def rope(y):
        x1r, x2r = y[0:quarter], y[quarter:2 * quarter]
        x1c, x2c = y[2 * quarter:3 * quarter], y[3 * quarter:]
        return jnp.concatenate(
            [x1r * cr - x2r * sr, x2r * cr + x1r * sr,
             x1c * cc - x2c * sc, x2c * cc + x1c * sc], axis=0)

    zeros = jnp.zeros((HEAD_DIM, CHUNK), BF16)
    ones = jnp.ones((ONES_ROWS, CHUNK), BF16)

    def finish_qa():
        for hh in range(A_Q_HEADS):
            q = rope(norm(OFF_QA + hh * HEAD_DIM, gqa_ref)).astype(BF16)
            slot = hh // A_GROUP
            qa_ref[0, hh, slot * HEAD_DIM:(slot + 1) * HEAD_DIM, :] = q
            qa_ref[0, hh, (1 - slot) * HEAD_DIM:(2 - slot) * HEAD_DIM, :] = zeros

    def finish_kva():
        kt = jnp.concatenate([rope(norm(OFF_KA + kv * HEAD_DIM, gka_ref))
                              for kv in range(A_KV_HEADS)], axis=0)
        ka_ref[0] = kt.T.astype(BF16)
        for kv in range(A_KV_HEADS):
            r0 = OFF_VA + kv * HEAD_DIM
            va_ref[0, kv, 0, 0:HEAD_DIM, :] = pt_ref[r0:r0 + HEAD_DIM, :].astype(BF16)
            va_ref[0, kv, 0, HEAD_DIM:HEAD_DIM + ONES_ROWS, :] = ones

    def finish_qb():
        for hb in range(B_HEADS):
            for c in range(2):
                q = norm(OFF_QB + (hb * 2 + c) * HEAD_DIM, gqb_ref).astype(BF16)
                qb_ref[0, hb, c, c * HEAD_DIM:(c + 1) * HEAD_DIM, :] = q
                qb_ref[0, hb, c, (1 - c) * HEAD_DIM:(2 - c) * HEAD_DIM, :] = zeros

    def finish_kb():
        for hb in range(B_HEADS):
            kt = jnp.concatenate([norm(OFF_KB + (hb * 2 + c) * HEAD_DIM, gkb_ref)
                                  for c in range(2)], axis=0)
            kb_ref[0, hb] = kt.T.astype(BF16)

    def finish_vb():
        for hb in range(B_HEADS):
            r0 = OFF_VB + hb * B_V_DIM
            vb_ref[0, hb, 0, 0:B_V_DIM, :] = pt_ref[r0:r0 + B_V_DIM, :].astype(BF16)
            vb_ref[0, hb, 0, B_V_DIM:B_V_DIM + ONES_ROWS, :] = ones

    project(OFF_QA, A_Q)
    project(OFF_KA, 2 * A_KV)
    finish_qa()
    project(OFF_QB, B_QK)
    finish_kva()
    project(OFF_KB, B_QK)
    finish_qb()
    project(OFF_VB, B_V)
    finish_kb()
    finish_vb()


def _in_proj(x, g, w_t, gqa, gka, gqb, gkb, cr, sr, cc, sc):
    bsz, seq, d = x.shape
    ns = seq // CHUNK
    const = lambda shape: pl.BlockSpec(shape, lambda b, s: (0,) * len(shape))
    tab = pl.BlockSpec((HEAD_DIM // 4, CHUNK), lambda b, s: (0, s))
    out_shape = (
        jax.ShapeDtypeStruct((bsz, A_Q_HEADS, 2 * HEAD_DIM, seq), BF16),
        jax.ShapeDtypeStruct((bsz, seq, 2 * HEAD_DIM), BF16),
        jax.ShapeDtypeStruct((bsz, A_KV_HEADS, ns, HEAD_DIM + ONES_ROWS, CHUNK), BF16),
        jax.ShapeDtypeStruct((bsz, B_HEADS, 2, 2 * HEAD_DIM, seq), BF16),
        jax.ShapeDtypeStruct((bsz, B_HEADS, seq, 2 * HEAD_DIM), BF16),
        jax.ShapeDtypeStruct((bsz, B_HEADS, ns, B_V_DIM + ONES_ROWS, CHUNK), BF16),
    )
    out_specs = (
        pl.BlockSpec((1, A_Q_HEADS, 2 * HEAD_DIM, CHUNK), lambda b, s: (b, 0, 0, s)),
        pl.BlockSpec((1, CHUNK, 2 * HEAD_DIM), lambda b, s: (b, s, 0)),
        pl.BlockSpec((1, A_KV_HEADS, 1, HEAD_DIM + ONES_ROWS, CHUNK),
                     lambda b, s: (b, 0, s, 0, 0)),
        pl.BlockSpec((1, B_HEADS, 2, 2 * HEAD_DIM, CHUNK), lambda b, s: (b, 0, 0, 0, s)),
        pl.BlockSpec((1, B_HEADS, CHUNK, 2 * HEAD_DIM), lambda b, s: (b, 0, s, 0)),
        pl.BlockSpec((1, B_HEADS, 1, B_V_DIM + ONES_ROWS, CHUNK),
                     lambda b, s: (b, 0, s, 0, 0)),
    )
    return pl.pallas_call(
        _in_proj_kernel,
        grid=(bsz, ns),
        in_specs=[
            pl.BlockSpec((1, CHUNK, d), lambda b, s: (b, s, 0)),
            const((1, d)),
            const((IN_WIDTH, d)),
            const((HEAD_DIM, CHUNK)), const((HEAD_DIM, CHUNK)),
            const((HEAD_DIM, CHUNK)), const((HEAD_DIM, CHUNK)),
            tab, tab, tab, tab,
        ],
        out_specs=out_specs,
        out_shape=out_shape,
        scratch_shapes=[pltpu.VMEM((IN_WIDTH, CHUNK), F32)],
        compiler_params=pltpu.CompilerParams(
            dimension_semantics=("arbitrary", "arbitrary"),
            vmem_limit_bytes=VMEM_LIMIT),
        name="in_proj",
    )(x, g, w_t, gqa, gka, gqb, gkb, cr, sr, cc, sc)


def _softmax_step(s, m, v_t, acc_ref, idx):
    m_new = jnp.maximum(m, jnp.max(s, axis=0, keepdims=True))
    alpha = jnp.exp2(m - m_new)
    p = jnp.exp2(s - m_new).astype(BF16)
    pv = jnp.dot(v_t, p, preferred_element_type=F32)
    acc_ref[idx] = alpha * acc_ref[idx] + pv
    return m_new


def _sums_in_range(sums):
    good = jnp.logical_and(sums >= SUM_MIN, sums <= SUM_MAX)
    return jnp.min(jnp.where(good, 1.0, 0.0)) > 0.5


def _attention_core(q_of, k_strip, v_strip, bias_of, acc_ref, p_ref, nk, zero, kstrip):
    n_maps, rows, tq = acc_ref.shape
    sum_row = rows - ONES_ROWS
    units = [(b0, st) for b0 in range(0, tq, QBLK) for st in range(CHUNK // kstrip)]
    cols = lambda b0: slice(b0, b0 + QBLK)
    keys = lambda st: slice(st * kstrip, (st + 1) * kstrip)

    def scores(j, b0, st):
        k = k_strip(j, st)
        bias = None if bias_of is None else bias_of(j, b0, st)
        out = []
        for c in range(n_maps):
            s = jnp.dot(k, q_of(c, cols(b0), st), preferred_element_type=F32)
            out.append(s if bias is None else s + bias)
        return out

    def produce(j, slot, b0, st):
        for c, s in enumerate(scores(j, b0, st)):
            p_ref[slot, c, keys(st), cols(b0)] = jnp.exp2(s).astype(BF16)

    def consume(j, slot, b0, st):
        v_t = v_strip(j, st)
        for c in range(n_maps):
            acc_ref[c, :, cols(b0)] += jnp.dot(v_t, p_ref[slot, c, keys(st), cols(b0)],
                                               preferred_element_type=F32)

    acc_ref[...] = jnp.zeros_like(acc_ref)

    for unit in units:
        produce(0, 0, *unit)

    def body(ib, final):
        for t in range(2 * FAST_UNROLL):
            j = 2 * FAST_UNROLL * ib + t
            for unit in units:
                if not (final and t == 2 * FAST_UNROLL - 1):
                    produce(j + 1, 1 - t % 2, *unit)
                consume(j, t % 2, *unit)

    n_bodies = nk // (2 * FAST_UNROLL)
    lax.fori_loop(0, zero + (n_bodies - 1), lambda ib, carry: body(ib, False) or carry, 0)
    body(n_bodies - 1, True)

    @pl.when(jnp.logical_not(_sums_in_range(acc_ref[:, sum_row:sum_row + 1, :])))
    def _():
        acc_ref[...] = jnp.zeros_like(acc_ref)

        blocks = sorted({(c, b0) for b0, _ in units for c in range(n_maps)})

        def step(j, ms):
            ms = dict(zip(blocks, ms))
            for b0, st in units:
                for c, s in enumerate(scores(j, b0, st)):
                    ms[c, b0] = _softmax_step(s, ms[c, b0], v_strip(j, st), acc_ref,
                                              (c, slice(None), cols(b0)))
            return tuple(ms[blk] for blk in blocks)

        lax.fori_loop(0, nk, step, (jnp.full((1, QBLK), -jnp.inf, F32),) * len(blocks))


def _strip_rows(j, strip, kstrip):
    return pl.ds(pl.multiple_of(j * CHUNK + strip * kstrip, kstrip), kstrip)


def _strip_cols(strip, kstrip):
    return slice(strip * kstrip, (strip + 1) * kstrip)


def _traced_zero():
    return jnp.minimum(pl.program_id(0), 0)


def _attn_a_kernel(q0_ref, q1_ref, k_ref, v_ref, o_ref, acc_ref, p_ref):
    q_refs = (q0_ref, q1_ref)
    _attention_core(
        q_of=lambda c, cols, st: q_refs[st][0, 0, :, cols],
        k_strip=lambda j, st: k_ref[0, _strip_rows(j, st, KSTRIP_A), :],
        v_strip=lambda j, st: v_ref[0, 0, j, :, _strip_cols(st, KSTRIP_A)],
        bias_of=None, acc_ref=acc_ref, p_ref=p_ref, nk=v_ref.shape[2],
        zero=_traced_zero(), kstrip=KSTRIP_A)
    acc = acc_ref[0]
    o_ref[0, 0] = (acc[0:HEAD_DIM] / acc[HEAD_DIM:HEAD_DIM + 1]).astype(BF16)


def _attn_a(qa, ka, va):
    bsz, _, _, seq = qa.shape
    ns = seq // CHUNK
    return pl.pallas_call(
        _attn_a_kernel,
        grid=(bsz, A_Q_HEADS, seq // TQ_A),
        in_specs=[
            pl.BlockSpec((1, 1, 2 * HEAD_DIM, TQ_A), lambda b, h, q: (b, h, 0, q)),
            pl.BlockSpec((1, 1, 2 * HEAD_DIM, TQ_A), lambda b, h, q: (b, h, 0, q)),
            pl.BlockSpec((1, seq, 2 * HEAD_DIM), lambda b, h, q: (b, 0, 0)),
            pl.BlockSpec((1, 1, ns, HEAD_DIM + ONES_ROWS, CHUNK),
                         lambda b, h, q: (b, h // A_GROUP, 0, 0, 0)),
        ],
        out_specs=pl.BlockSpec((1, 1, HEAD_DIM, TQ_A), lambda b, h, q: (b, h, 0, q)),
        out_shape=jax.ShapeDtypeStruct((bsz, A_Q_HEADS, HEAD_DIM, seq), BF16),
        scratch_shapes=[pltpu.VMEM((1, HEAD_DIM + ONES_ROWS, TQ_A), F32),
                        pltpu.VMEM((2, 1, CHUNK, TQ_A), BF16)],
        compiler_params=pltpu.CompilerParams(
            dimension_semantics=("arbitrary", "arbitrary", "arbitrary"),
            vmem_limit_bytes=VMEM_LIMIT),
        name="attn_a",
    )(qa, qa, ka, va)


def _attn_b_kernel(q_ref, k_ref, v_ref, bias_ref, lq1_ref, lk1_ref, lq2_ref,
                   lk2_ref, g_ref, o_ref, acc_ref, p_ref, *, lambda_init):
    qi = pl.program_id(2)
    far = BIAS_KINDS // 2
    first_block = qi * (TQ_B // CHUNK)
    _attention_core(
        q_of=lambda c, cols, st: q_ref[0, 0, c, :, cols],
        k_strip=lambda j, st: k_ref[0, 0, _strip_rows(j, st, KSTRIP_B), :],
        v_strip=lambda j, st: v_ref[0, 0, j, :, _strip_cols(st, KSTRIP_B)],
        bias_of=lambda j, b0, st: bias_ref[
            0, jnp.clip(j - (first_block + b0 // CHUNK), -far, far) + far,
            _strip_cols(st, KSTRIP_B), :],
        acc_ref=acc_ref, p_ref=p_ref, nk=v_ref.shape[2], zero=_traced_zero(),
        kstrip=KSTRIP_B)

    lam = (jnp.exp(jnp.sum(lq1_ref[...] * lk1_ref[...], keepdims=True))
           - jnp.exp(jnp.sum(lq2_ref[...] * lk2_ref[...], keepdims=True))
           + lambda_init)
    a0, a1 = acc_ref[0], acc_ref[1]
    o = (a0[0:B_V_DIM] / a0[B_V_DIM:B_V_DIM + 1]
         - lam * (a1[0:B_V_DIM] / a1[B_V_DIM:B_V_DIM + 1]))
    msq = jnp.mean(o * o, axis=0, keepdims=True)
    o_ref[0, 0] = (o * lax.rsqrt(msq + EPS) * g_ref[...]).astype(BF16)


def _attn_b(qb, kb, vb, bias, lq1, lk1, lq2, lk2, gsub, lambda_init):
    bsz, _, _, _, seq = qb.shape
    ns = seq // CHUNK
    vec = pl.BlockSpec((1, HEAD_DIM), lambda b, h, q: (0, 0))
    return pl.pallas_call(
        functools.partial(_attn_b_kernel, lambda_init=lambda_init),
        grid=(bsz, B_HEADS, seq // TQ_B),
        in_specs=[
            pl.BlockSpec((1, 1, 2, 2 * HEAD_DIM, TQ_B), lambda b, h, q: (b, h, 0, 0, q)),
            pl.BlockSpec((1, 1, seq, 2 * HEAD_DIM), lambda b, h, q: (b, h, 0, 0)),
            pl.BlockSpec((1, 1, ns, B_V_DIM + ONES_ROWS, CHUNK),
                         lambda b, h, q: (b, h, 0, 0, 0)),
            pl.BlockSpec((1, BIAS_KINDS, CHUNK, CHUNK), lambda b, h, q: (h, 0, 0, 0)),
            vec, vec, vec, vec,
            pl.BlockSpec((B_V_DIM, TQ_B), lambda b, h, q: (0, 0)),
        ],
        out_specs=pl.BlockSpec((1, 1, B_V_DIM, TQ_B), lambda b, h, q: (b, h, 0, q)),
        out_shape=jax.ShapeDtypeStruct((bsz, B_HEADS, B_V_DIM, seq), BF16),
        scratch_shapes=[pltpu.VMEM((2, B_V_DIM + ONES_ROWS, TQ_B), F32),
                        pltpu.VMEM((2, 2, CHUNK, TQ_B), BF16)],
        compiler_params=pltpu.CompilerParams(
            dimension_semantics=("arbitrary", "arbitrary", "arbitrary"),
            vmem_limit_bytes=VMEM_LIMIT),
        name="attn_b",
    )(qb, kb, vb, bias, lq1, lk1, lq2, lk2, gsub)


def _out_mlp_kernel(x_ref, oa_ref, ob_ref, wo_ref, g_ref, wu_ref, wd_ref, y_ref):
    tn = (((0,), (0,)), ((), ()))
    attn = lax.dot_general(oa_ref[0], wo_ref[0:A_Q, :], tn, preferred_element_type=F32)
    attn = attn + lax.dot_general(ob_ref[0], wo_ref[A_Q:A_Q + B_V, :], tn,
                                  preferred_element_type=F32)
    x1 = x_ref[0] + attn
    ms = jnp.mean(x1 * x1, axis=-1, keepdims=True)
    h = (x1 * lax.rsqrt(ms + EPS) * g_ref[...]).astype(BF16)
    acc = x1
    for c in range(wu_ref.shape[1] // MLP_FCHUNK):
        cols = slice(c * MLP_FCHUNK, (c + 1) * MLP_FCHUNK)
        u = jnp.dot(h, wu_ref[:, cols], preferred_element_type=F32)
        u = jnp.square(jnp.maximum(u, 0.0)).astype(BF16)
        acc = acc + jnp.dot(u, wd_ref[cols, :], preferred_element_type=F32)
    y_ref[0] = acc


def _out_mlp(x, oa, ob, w_out, g2, w_up, w_down):
    bsz, seq, d = x.shape
    d_ff = w_up.shape[1]
    resident = lambda shape: pl.BlockSpec(shape, lambda b, s: (0,) * len(shape),
                                          pipeline_mode=pl.Buffered(1))
    return pl.pallas_call(
        _out_mlp_kernel,
        grid=(bsz, seq // MLP_ROWS),
        in_specs=[
            pl.BlockSpec((1, MLP_ROWS, d), lambda b, s: (b, s, 0)),
            pl.BlockSpec((1, A_Q, MLP_ROWS), lambda b, s: (b, 0, s)),
            pl.BlockSpec((1, B_V, MLP_ROWS), lambda b, s: (b, 0, s)),
            resident((A_Q + B_V, d)),
            resident((1, d)),
            resident((d, d_ff)),
            resident((d_ff, d)),
        ],
        out_specs=pl.BlockSpec((1, MLP_ROWS, d), lambda b, s: (b, s, 0)),
        out_shape=jax.ShapeDtypeStruct((bsz, seq, d), F32),
        compiler_params=pltpu.CompilerParams(
            dimension_semantics=("arbitrary", "arbitrary"),
            vmem_limit_bytes=VMEM_LIMIT),
        name="out_mlp",
    )(x, oa, ob, w_out, g2, w_up, w_down)


def _rope_tables_t(seq):
    pos = jnp.arange(seq)
    row = (pos // GRID_W).astype(F32)
    col = (pos % GRID_W).astype(F32)
    half = HEAD_DIM // 2
    inv_freq = 1.0 / (ROPE_THETA ** (jnp.arange(0, half, 2, dtype=F32) / half))
    ang_r = inv_freq[:, None] * row[None, :]
    ang_c = inv_freq[:, None] * col[None, :]
    return jnp.cos(ang_r), jnp.sin(ang_r), jnp.cos(ang_c), jnp.sin(ang_c)


def kernel(x, attn_norm_g, w_in, a_q_norm_g, a_k_norm_g, b_q_norm_g, b_k_norm_g,
           lambda_q1, lambda_k1, lambda_q2, lambda_k2, b_subln_g, rel_bias,
           w_out, mlp_norm_g, w_up, w_down):
    bsz, seq, d = x.shape
    depth = w_in.shape[0]
    assert seq % CHUNK == 0 and seq % TQ_A == 0 and seq % MLP_ROWS == 0
    assert w_in.shape[2] == IN_WIDTH and w_up.shape[2] % MLP_FCHUNK == 0
    cr, sr, cc, sc = _rope_tables_t(seq)
    qscale = HEAD_DIM ** -0.5 * LOG2E
    lanes = lambda v, n: jnp.broadcast_to(v.astype(F32)[:, None], (v.shape[0], n))
    bias = _bias_tiles(rel_bias.astype(F32).T)
    for l in range(depth):
        lambda_init = 0.8 - 0.6 * math.exp(-0.3 * l)
        qa, ka, va, qb, kb, vb = _in_proj(
            x, attn_norm_g[l].astype(F32)[None, :], w_in[l].T.astype(BF16),
            lanes(a_q_norm_g[l] * qscale, CHUNK), lanes(a_k_norm_g[l], CHUNK),
            lanes(b_q_norm_g[l] * qscale, CHUNK), lanes(b_k_norm_g[l], CHUNK),
            cr, sr, cc, sc)
        oa = _attn_a(qa, ka, va)
        ob = _attn_b(qb, kb, vb, bias,
                     lambda_q1[l].astype(F32)[None, :], lambda_k1[l].astype(F32)[None, :],
                     lambda_q2[l].astype(F32)[None, :], lambda_k2[l].astype(F32)[None, :],
                     lanes(b_subln_g[l] * (1.0 - lambda_init), TQ_B), lambda_init)
        x = _out_mlp(x, oa.reshape(bsz, A_Q, seq), ob.reshape(bsz, B_V, seq),
                     w_out[l].astype(BF16), mlp_norm_g[l].astype(F32)[None, :],
                     w_up[l].astype(BF16), w_down[l].astype(BF16))
    return x
```

```python
import functools
import math

import jax
import jax.numpy as jnp
from jax import lax
from jax.experimental import pallas as pl
from jax.experimental.pallas import tpu as pltpu

HEAD_DIM = 64
A_Q_HEADS = 8
A_KV_HEADS = 2
A_GROUP = A_Q_HEADS // A_KV_HEADS
B_HEADS = 4
B_V_DIM = 2 * HEAD_DIM
A_Q = A_Q_HEADS * HEAD_DIM
A_KV = A_KV_HEADS * HEAD_DIM
B_QK = B_HEADS * 2 * HEAD_DIM
B_V = B_HEADS * B_V_DIM
GRID_W = 64
ROPE_THETA = 10000.0
REL_BUCKETS = 32
EPS = 1e-6
LOG2E = 1.4426950408889634
T5_BUCKET_STARTS = tuple(range(9)) + (12, 16, 23, 32, 46, 64, 91)

OFF_QA = 0
OFF_KA = OFF_QA + A_Q
OFF_VA = OFF_KA + A_KV
OFF_QB = OFF_VA + A_KV
OFF_KB = OFF_QB + B_QK
OFF_VB = OFF_KB + B_QK
IN_WIDTH = OFF_VB + B_V

BF16_SUBLANES = 16
ONES_ROWS = BF16_SUBLANES
CHUNK = 512
TQ_A = 8192
TQ_B = 2048
BIAS_KINDS = 5
MLP_ROWS = 512
MLP_FCHUNK = 1024
VMEM_LIMIT = 48 * 1024 * 1024
FAST_UNROLL = 1
KSTRIP_A = 256
KSTRIP_B = CHUNK
QBLK = CHUNK
SUM_MIN = 2.0 ** -64
SUM_MAX = 2.0 ** 64

BF16 = jnp.bfloat16
F32 = jnp.float32


def _bias_tile_kernel(tbl_ref, out_ref):
    h = pl.program_id(0)
    d = pl.program_id(1) - BIAS_KINDS // 2
    kk = lax.broadcasted_iota(jnp.int32, (CHUNK, CHUNK), 0)
    qq = lax.broadcasted_iota(jnp.int32, (CHUNK, CHUNK), 1)
    rel = d * CHUNK + kk - qq
    n = jnp.abs(rel)

    def side(first_bucket):
        v = jnp.full((CHUNK, CHUNK), tbl_ref[h, first_bucket] * LOG2E, F32)
        for b, start in enumerate(T5_BUCKET_STARTS[1:], 1):
            v = jnp.where(n >= start, tbl_ref[h, first_bucket + b] * LOG2E, v)
        return v

    @pl.when(d < 0)
    def _():
        out_ref[0, 0] = side(0)

    @pl.when(d > 0)
    def _():
        out_ref[0, 0] = side(REL_BUCKETS // 2)

    @pl.when(d == 0)
    def _():
        out_ref[0, 0] = jnp.where(rel > 0, side(REL_BUCKETS // 2), side(0))


def _bias_tiles(rel_bias_t):
    return pl.pallas_call(
        _bias_tile_kernel,
        grid=(B_HEADS, BIAS_KINDS),
        in_specs=[pl.BlockSpec(memory_space=pltpu.SMEM)],
        out_specs=pl.BlockSpec((1, 1, CHUNK, CHUNK), lambda h, d: (h, d, 0, 0)),
        out_shape=jax.ShapeDtypeStruct((B_HEADS, BIAS_KINDS, CHUNK, CHUNK), F32),
        name="bias_tiles",
    )(rel_bias_t)


def _in_proj_kernel(x_ref, g_ref, w_ref, gqa_ref, gka_ref, gqb_ref, gkb_ref,
                    cr_ref, sr_ref, cc_ref, sc_ref,
                    qa_ref, ka_ref, va_ref, qb_ref, kb_ref, vb_ref, pt_ref):
    x = x_ref[0]
    ms = jnp.mean(x * x, axis=-1, keepdims=True)
    h = (x * lax.rsqrt(ms + EPS) * g_ref[...]).astype(BF16)
    cr, sr, cc, sc = cr_ref[...], sr_ref[...], cc_ref[...], sc_ref[...]
    quarter = HEAD_DIM // 4

    def project(r0, n):
        pt_ref[r0:r0 + n, :] = lax.dot_general(
            w_ref[r0:r0 + n, :], h, (((1,), (1,)), ((), ())), preferred_element_type=F32)

    def norm(r0, gain_ref):
        t = pt_ref[r0:r0 + HEAD_DIM, :]
        m = jnp.mean(t * t, axis=0, keepdims=True)
        return t * lax.rsqrt(m + EPS) * gain_ref[...]

    def rope(y):
        x1r, x2r = y[0:quarter], y[quarter:2 * quarter]
        x1c, x2c = y[2 * quarter:3 * quarter], y[3 * quarter:]
        return jnp.concatenate(
            [x1r * cr - x2r * sr, x2r * cr + x1r * sr,
             x1c * cc - x2c * sc, x2c * cc + x1c * sc], axis=0)

    zeros = jnp.zeros((HEAD_DIM, CHUNK), BF16)
    ones = jnp.ones((ONES_ROWS, CHUNK), BF16)

    def finish_qa():
        for hh in range(A_Q_HEADS):
            q = rope(norm(OFF_QA + hh * HEAD_DIM, gqa_ref)).astype(BF16)
            slot = hh // A_GROUP
            qa_ref[0, hh, slot * HEAD_DIM:(slot + 1) * HEAD_DIM, :] = q
            qa_ref[0, hh, (1 - slot) * HEAD_DIM:(2 - slot) * HEAD_DIM, :] = zeros

    def finish_kva():
        kt = jnp.concatenate([rope(norm(OFF_KA + kv * HEAD_DIM, gka_ref))
                              for kv in range(A_KV_HEADS)], axis=0)
        ka_ref[0] = kt.T.astype(BF16)
        for kv in range(A_KV_HEADS):
            r0 = OFF_VA + kv * HEAD_DIM
            va_ref[0, kv, 0, 0:HEAD_DIM, :] = pt_ref[r0:r0 + HEAD_DIM, :].astype(BF16)
            va_ref[0, kv, 0, HEAD_DIM:HEAD_DIM + ONES_ROWS, :] = ones

    def finish_qb():
        for hb in range(B_HEADS):
            for c in range(2):
                q = norm(OFF_QB + (hb * 2 + c) * HEAD_DIM, gqb_ref).astype(BF16)
                qb_ref[0, hb, c, c * HEAD_DIM:(c + 1) * HEAD_DIM, :] = q
                qb_ref[0, hb, c, (1 - c) * HEAD_DIM:(2 - c) * HEAD_DIM, :] = zeros

    def finish_kb():
        for hb in range(B_HEADS):
            kt = jnp.concatenate([norm(OFF_KB + (hb * 2 + c) * HEAD_DIM, gkb_ref)
                                  for c in range(2)], axis=0)
            kb_ref[0, hb] = kt.T.astype(BF16)

    def finish_vb():
        for hb in range(B_HEADS):
            r0 = OFF_VB + hb * B_V_DIM
            vb_ref[0, hb, 0, 0:B_V_DIM, :] = pt_ref[r0:r0 + B_V_DIM, :].astype(BF16)
            vb_ref[0, hb, 0, B_V_DIM:B_V_DIM + ONES_ROWS, :] = ones

    project(OFF_QA, A_Q)
    project(OFF_KA, 2 * A_KV)
    finish_qa()
    project(OFF_QB, B_QK)
    finish_kva()
    project(OFF_KB, B_QK)
    finish_qb()
    project(OFF_VB, B_V)
    finish_kb()
    finish_vb()


def _in_proj(x, g, w_t, gqa, gka, gqb, gkb, cr, sr, cc, sc):
    bsz, seq, d = x.shape
    ns = seq // CHUNK
    const = lambda shape: pl.BlockSpec(shape, lambda b, s: (0,) * len(shape))
    tab = pl.BlockSpec((HEAD_DIM // 4, CHUNK), lambda b, s: (0, s))
    out_shape = (
        jax.ShapeDtypeStruct((bsz, A_Q_HEADS, 2 * HEAD_DIM, seq), BF16),
        jax.ShapeDtypeStruct((bsz, seq, 2 * HEAD_DIM), BF16),
        jax.ShapeDtypeStruct((bsz, A_KV_HEADS, ns, HEAD_DIM + ONES_ROWS, CHUNK), BF16),
        jax.ShapeDtypeStruct((bsz, B_HEADS, 2, 2 * HEAD_DIM, seq), BF16),
        jax.ShapeDtypeStruct((bsz, B_HEADS, seq, 2 * HEAD_DIM), BF16),
        jax.ShapeDtypeStruct((bsz, B_HEADS, ns, B_V_DIM + ONES_ROWS, CHUNK), BF16),
    )
    out_specs = (
        pl.BlockSpec((1, A_Q_HEADS, 2 * HEAD_DIM, CHUNK), lambda b, s: (b, 0, 0, s)),
        pl.BlockSpec((1, CHUNK, 2 * HEAD_DIM), lambda b, s: (b, s, 0)),
        pl.BlockSpec((1, A_KV_HEADS, 1, HEAD_DIM + ONES_ROWS, CHUNK),
                     lambda b, s: (b, 0, s, 0, 0)),
        pl.BlockSpec((1, B_HEADS, 2, 2 * HEAD_DIM, CHUNK), lambda b, s: (b, 0, 0, 0, s)),
        pl.BlockSpec((1, B_HEADS, CHUNK, 2 * HEAD_DIM), lambda b, s: (b, 0, s, 0)),
        pl.BlockSpec((1, B_HEADS, 1, B_V_DIM + ONES_ROWS, CHUNK),
                     lambda b, s: (b, 0, s, 0, 0)),
    )
    return pl.pallas_call(
        _in_proj_kernel,
        grid=(bsz, ns),
        in_specs=[
            pl.BlockSpec((1, CHUNK, d), lambda b, s: (b, s, 0)),
            const((1, d)),
            const((IN_WIDTH, d)),
            const((HEAD_DIM, CHUNK)), const((HEAD_DIM, CHUNK)),
            const((HEAD_DIM, CHUNK)), const((HEAD_DIM, CHUNK)),
            tab, tab, tab, tab,
        ],
        out_specs=out_specs,
        out_shape=out_shape,
        scratch_shapes=[pltpu.VMEM((IN_WIDTH, CHUNK), F32)],
        compiler_params=pltpu.CompilerParams(
            dimension_semantics=("arbitrary", "arbitrary"),
            vmem_limit_bytes=VMEM_LIMIT),
        name="in_proj",
    )(x, g, w_t, gqa, gka, gqb, gkb, cr, sr, cc, sc)


def _softmax_step(s, m, v_t, acc_ref, idx):
    m_new = jnp.maximum(m, jnp.max(s, axis=0, keepdims=True))
    alpha = jnp.exp2(m - m_new)
    p = jnp.exp2(s - m_new).astype(BF16)
    pv = jnp.dot(v_t, p, preferred_element_type=F32)
    acc_ref[idx] = alpha * acc_ref[idx] + pv
    return m_new


def _sums_in_range(sums):
    good = jnp.logical_and(sums >= SUM_MIN, sums <= SUM_MAX)
    return jnp.min(jnp.where(good, 1.0, 0.0)) > 0.5


def _attention_core(q_of, k_strip, v_strip, bias_of, acc_ref, p_ref, nk, zero, kstrip):
    n_maps, rows, tq = acc_ref.shape
    sum_row = rows - ONES_ROWS
    units = [(b0, st) for b0 in range(0, tq, QBLK) for st in range(CHUNK // kstrip)]
    cols = lambda b0: slice(b0, b0 + QBLK)
    keys = lambda st: slice(st * kstrip, (st + 1) * kstrip)

    def scores(j, b0, st):
        k = k_strip(j, st)
        bias = None if bias_of is None else bias_of(j, b0, st)
        out = []
        for c in range(n_maps):
            s = jnp.dot(k, q_of(c, cols(b0), st), preferred_element_type=F32)
            out.append(s if bias is None else s + bias)
        return out

    def produce(j, slot, b0, st):
        for c, s in enumerate(scores(j, b0, st)):
            p_ref[slot, c, keys(st), cols(b0)] = jnp.exp2(s).astype(BF16)

    def consume(j, slot, b0, st):
        v_t = v_strip(j, st)
        for c in range(n_maps):
            acc_ref[c, :, cols(b0)] += jnp.dot(v_t, p_ref[slot, c, keys(st), cols(b0)],
                                               preferred_element_type=F32)

    acc_ref[...] = jnp.zeros_like(acc_ref)

    for unit in units:
        produce(0, 0, *unit)

    def body(ib, final):
        for t in range(2 * FAST_UNROLL):
            j = 2 * FAST_UNROLL * ib + t
            for unit in units:
                if not (final and t == 2 * FAST_UNROLL - 1):
                    produce(j + 1, 1 - t % 2, *unit)
                consume(j, t % 2, *unit)

    n_bodies = nk // (2 * FAST_UNROLL)
    lax.fori_loop(0, zero + (n_bodies - 1), lambda ib, carry: body(ib, False) or carry, 0)
    body(n_bodies - 1, True)

    @pl.when(jnp.logical_not(_sums_in_range(acc_ref[:, sum_row:sum_row + 1, :])))
    def _():
        acc_ref[...] = jnp.zeros_like(acc_ref)

        blocks = sorted({(c, b0) for b0, _ in units for c in range(n_maps)})

        def step(j, ms):
            ms = dict(zip(blocks, ms))
            for b0, st in units:
                for c, s in enumerate(scores(j, b0, st)):
                    ms[c, b0] = _softmax_step(s, ms[c, b0], v_strip(j, st), acc_ref,
                                              (c, slice(None), cols(b0)))
            return tuple(ms[blk] for blk in blocks)

        lax.fori_loop(0, nk, step, (jnp.full((1, QBLK), -jnp.inf, F32),) * len(blocks))


def _strip_rows(j, strip, kstrip):
    return pl.ds(pl.multiple_of(j * CHUNK + strip * kstrip, kstrip), kstrip)


def _strip_cols(strip, kstrip):
    return slice(strip * kstrip, (strip + 1) * kstrip)


def _traced_zero():
    return jnp.minimum(pl.program_id(0), 0)


def _attn_a_kernel(q0_ref, q1_ref, k_ref, v_ref, o_ref, acc_ref, p_ref):
    q_refs = (q0_ref, q1_ref)
    _attention_core(
        q_of=lambda c, cols, st: q_refs[st][0, 0, :, cols],
        k_strip=lambda j, st: k_ref[0, _strip_rows(j, st, KSTRIP_A), :],
        v_strip=lambda j, st: v_ref[0, 0, j, :, _strip_cols(st, KSTRIP_A)],
        bias_of=None, acc_ref=acc_ref, p_ref=p_ref, nk=v_ref.shape[2],
        zero=_traced_zero(), kstrip=KSTRIP_A)
    acc = acc_ref[0]
    o_ref[0, 0] = (acc[0:HEAD_DIM] / acc[HEAD_DIM:HEAD_DIM + 1]).astype(BF16)


def _attn_a(qa, ka, va):
    bsz, _, _, seq = qa.shape
    ns = seq // CHUNK
    return pl.pallas_call(
        _attn_a_kernel,
        grid=(bsz, A_Q_HEADS, seq // TQ_A),
        in_specs=[
            pl.BlockSpec((1, 1, 2 * HEAD_DIM, TQ_A), lambda b, h, q: (b, h, 0, q)),
            pl.BlockSpec((1, 1, 2 * HEAD_DIM, TQ_A), lambda b, h, q: (b, h, 0, q)),
            pl.BlockSpec((1, seq, 2 * HEAD_DIM), lambda b, h, q: (b, 0, 0)),
            pl.BlockSpec((1, 1, ns, HEAD_DIM + ONES_ROWS, CHUNK),
                         lambda b, h, q: (b, h // A_GROUP, 0, 0, 0)),
        ],
        out_specs=pl.BlockSpec((1, 1, HEAD_DIM, TQ_A), lambda b, h, q: (b, h, 0, q)),
        out_shape=jax.ShapeDtypeStruct((bsz, A_Q_HEADS, HEAD_DIM, seq), BF16),
        scratch_shapes=[pltpu.VMEM((1, HEAD_DIM + ONES_ROWS, TQ_A), F32),
                        pltpu.VMEM((2, 1, CHUNK, TQ_A), BF16)],
        compiler_params=pltpu.CompilerParams(
            dimension_semantics=("arbitrary", "arbitrary", "arbitrary"),
            vmem_limit_bytes=VMEM_LIMIT),
        name="attn_a",
    )(qa, qa, ka, va)


def _attn_b_kernel(q_ref, k_ref, v_ref, bias_ref, lq1_ref, lk1_ref, lq2_ref,
                   lk2_ref, g_ref, o_ref, acc_ref, p_ref, *, lambda_init):
    qi = pl.program_id(2)
    far = BIAS_KINDS // 2
    first_block = qi * (TQ_B // CHUNK)
    _attention_core(
        q_of=lambda c, cols, st: q_ref[0, 0, c, :, cols],
        k_strip=lambda j, st: k_ref[0, 0, _strip_rows(j, st, KSTRIP_B), :],
        v_strip=lambda j, st: v_ref[0, 0, j, :, _strip_cols(st, KSTRIP_B)],
        bias_of=lambda j, b0, st: bias_ref[
            0, jnp.clip(j - (first_block + b0 // CHUNK), -far, far) + far,
            _strip_cols(st, KSTRIP_B), :],
        acc_ref=acc_ref, p_ref=p_ref, nk=v_ref.shape[2], zero=_traced_zero(),
        kstrip=KSTRIP_B)

    lam = (jnp.exp(jnp.sum(lq1_ref[...] * lk1_ref[...], keepdims=True))
           - jnp.exp(jnp.sum(lq2_ref[...] * lk2_ref[...], keepdims=True))
           + lambda_init)
    a0, a1 = acc_ref[0], acc_ref[1]
    o = (a0[0:B_V_DIM] / a0[B_V_DIM:B_V_DIM + 1]
         - lam * (a1[0:B_V_DIM] / a1[B_V_DIM:B_V_DIM + 1]))
    msq = jnp.mean(o * o, axis=0, keepdims=True)
    o_ref[0, 0] = (o * lax.rsqrt(msq + EPS) * g_ref[...]).astype(BF16)


def _attn_b(qb, kb, vb, bias, lq1, lk1, lq2, lk2, gsub, lambda_init):
    bsz, _, _, _, seq = qb.shape
    ns = seq // CHUNK
    vec = pl.BlockSpec((1, HEAD_DIM), lambda b, h, q: (0, 0))
    return pl.pallas_call(
        functools.partial(_attn_b_kernel, lambda_init=lambda_init),
        grid=(bsz, B_HEADS, seq // TQ_B),
        in_specs=[
            pl.BlockSpec((1, 1, 2, 2 * HEAD_DIM, TQ_B), lambda b, h, q: (b, h, 0, 0, q)),
            pl.BlockSpec((1, 1, seq, 2 * HEAD_DIM), lambda b, h, q: (b, h, 0, 0)),
            pl.BlockSpec((1, 1, ns, B_V_DIM + ONES_ROWS, CHUNK),
                         lambda b, h, q: (b, h, 0, 0, 0)),
            pl.BlockSpec((1, BIAS_KINDS, CHUNK, CHUNK), lambda b, h, q: (h, 0, 0, 0)),
            vec, vec, vec, vec,
            pl.BlockSpec((B_V_DIM, TQ_B), lambda b, h, q: (0, 0)),
        ],
        out_specs=pl.BlockSpec((1, 1, B_V_DIM, TQ_B), lambda b, h, q: (b, h, 0, q)),
        out_shape=jax.ShapeDtypeStruct((bsz, B_HEADS, B_V_DIM, seq), BF16),
        scratch_shapes=[pltpu.VMEM((2, B_V_DIM + ONES_ROWS, TQ_B), F32),
                        pltpu.VMEM((2, 2, CHUNK, TQ_B), BF16)],
        compiler_params=pltpu.CompilerParams(
            dimension_semantics=("arbitrary", "arbitrary", "arbitrary"),
            vmem_limit_bytes=VMEM_LIMIT),
        name="attn_b",
    )(qb, kb, vb, bias, lq1, lk1, lq2, lk2, gsub)


def _out_mlp_kernel(x_ref, oa_ref, ob_ref, wo_ref, g_ref, wu_ref, wd_ref, y_ref):
    tn = (((0,), (0,)), ((), ()))
    attn = lax.dot_general(oa_ref[0], wo_ref[0:A_Q, :], tn, preferred_element_type=F32)
    attn = attn + lax.dot_general(ob_ref[0], wo_ref[A_Q:A_Q + B_V, :], tn,
                                  preferred_element_type=F32)
    x1 = x_ref[0] + attn
    ms = jnp.mean(x1 * x1, axis=-1, keepdims=True)
    h = (x1 * lax.rsqrt(ms + EPS) * g_ref[...]).astype(BF16)
    acc = x1
    for c in range(wu_ref.shape[1] // MLP_FCHUNK):
        cols = slice(c * MLP_FCHUNK, (c + 1) * MLP_FCHUNK)
        u = jnp.dot(h, wu_ref[:, cols], preferred_element_type=F32)
        u = jnp.square(jnp.maximum(u, 0.0)).astype(BF16)
        acc = acc + jnp.dot(u, wd_ref[cols, :], preferred_element_type=F32)
    y_ref[0] = acc


def _out_mlp(x, oa, ob, w_out, g2, w_up, w_down):
    bsz, seq, d = x.shape
    d_ff = w_up.shape[1]
    resident = lambda shape: pl.BlockSpec(shape, lambda b, s: (0,) * len(shape),
                                          pipeline_mode=pl.Buffered(1))
    return pl.pallas_call(
        _out_mlp_kernel,
        grid=(bsz, seq // MLP_ROWS),
        in_specs=[
            pl.BlockSpec((1, MLP_ROWS, d), lambda b, s: (b, s, 0)),
            pl.BlockSpec((1, A_Q, MLP_ROWS), lambda b, s: (b, 0, s)),
            pl.BlockSpec((1, B_V, MLP_ROWS), lambda b, s: (b, 0, s)),
            resident((A_Q + B_V, d)),
            resident((1, d)),
            resident((d, d_ff)),
            resident((d_ff, d)),
        ],
        out_specs=pl.BlockSpec((1, MLP_ROWS, d), lambda b, s: (b, s, 0)),
        out_shape=jax.ShapeDtypeStruct((bsz, seq, d), F32),
        compiler_params=pltpu.CompilerParams(
            dimension_semantics=("arbitrary", "arbitrary"),
            vmem_limit_bytes=VMEM_LIMIT),
        name="out_mlp",
    )(x, oa, ob, w_out, g2, w_up, w_down)


def _rope_tables_t(seq):
    pos = jnp.arange(seq)
    row = (pos // GRID_W).astype(F32)
    col = (pos % GRID_W).astype(F32)
    half = HEAD_DIM // 2
    inv_freq = 1.0 / (ROPE_THETA ** (jnp.arange(0, half, 2, dtype=F32) / half))
    ang_r = inv_freq[:, None] * row[None, :]
    ang_c = inv_freq[:, None] * col[None, :]
    return jnp.cos(ang_r), jnp.sin(ang_r), jnp.cos(ang_c), jnp.sin(ang_c)


def kernel(x, attn_norm_g, w_in, a_q_norm_g, a_k_norm_g, b_q_norm_g, b_k_norm_g,
           lambda_q1, lambda_k1, lambda_q2, lambda_k2, b_subln_g, rel_bias,
           w_out, mlp_norm_g, w_up, w_down):
    bsz, seq, d = x.shape
    depth = w_in.shape[0]
    assert seq % CHUNK == 0 and seq % TQ_A == 0 and seq % MLP_ROWS == 0
    assert w_in.shape[2] == IN_WIDTH and w_up.shape[2] % MLP_FCHUNK == 0
    cr, sr, cc, sc = _rope_tables_t(seq)
    qscale = HEAD_DIM ** -0.5 * LOG2E
    lanes = lambda v, n: jnp.broadcast_to(v.astype(F32)[:, None], (v.shape[0], n))
    bias = _bias_tiles(rel_bias.astype(F32).T)
    for l in range(depth):
        lambda_init = 0.8 - 0.6 * math.exp(-0.3 * l)
        qa, ka, va, qb, kb, vb = _in_proj(
            x, attn_norm_g[l].astype(F32)[None, :], w_in[l].T.astype(BF16),
            lanes(a_q_norm_g[l] * qscale, CHUNK), lanes(a_k_norm_g[l], CHUNK),
            lanes(b_q_norm_g[l] * qscale, CHUNK), lanes(b_k_norm_g[l], CHUNK),
            cr, sr, cc, sc)
        oa = _attn_a(qa, ka, va)
        ob = _attn_b(qb, kb, vb, bias,
                     lambda_q1[l].astype(F32)[None, :], lambda_k1[l].astype(F32)[None, :],
                     lambda_q2[l].astype(F32)[None, :], lambda_k2[l].astype(F32)[None, :],
                     lanes(b_subln_g[l] * (1.0 - lambda_init), TQ_B), lambda_init)
        x = _out_mlp(x, oa.reshape(bsz, A_Q, seq), ob.reshape(bsz, B_V, seq),
                     w_out[l].astype(BF16), mlp_norm_g[l].astype(F32)[None, :],
                     w_up[l].astype(BF16), w_down[l].astype(BF16))
    return x
```
